```python
import math
import jax, jax.numpy as jnp
from jax import lax
import numpy as np

D_MODEL = 2048
BATCH = 16
SEQ = 2048
DEPTH = 2

BW = D_MODEL // 2
A_HEADS = 4
A_HEAD_DIM = BW // A_HEADS
A_CONV = 4
A_CHUNK = 128
B_WINDOWS = (2, 4, 8, 16)
B_GROUPS = len(B_WINDOWS)
B_GROUP_DIM = BW // B_GROUPS
C_HEADS = 8
C_HEAD_DIM = BW // C_HEADS
C_BLOCK = 128
N_BRANCH = 3
OFF_A = 0
N_A = 4 * BW + 2 * A_HEADS
OFF_B = OFF_A + N_A
N_B = BW
OFF_C = OFF_B + N_B
N_C = 3 * BW + C_HEADS
OFF_G = OFF_C + N_C
N_G = N_BRANCH * D_MODEL
IN_WIDTH = OFF_G + N_G
N_EXPERTS = 32
N_GROUPS = 8
EXPERTS_PER_GROUP = N_EXPERTS // N_GROUPS
TOP_K = 2
D_FF = D_MODEL * 3 // 8
ALPHA = (2 * DEPTH) ** 0.25
BETA = (8 * DEPTH) ** -0.25
LN_EPS = 1e-5

kernel_name = "hybrid_mlstm_pool_fox_grouped_moe_deepnorm"


def layer_norm(x, g, b):
    xf = x.astype(jnp.float32)
    mu = xf.mean(-1, keepdims=True)
    var = jnp.square(xf - mu).mean(-1, keepdims=True)
    return ((xf - mu) * lax.rsqrt(var + LN_EPS) * g + b).astype(x.dtype)


def causal_dwconv(u, w):
    K = w.shape[0]
    return lax.conv_general_dilated(
        u, w[:, None, :].astype(u.dtype), window_strides=(1,), padding=[(K - 1, 0)],
        dimension_numbers=('NWC', 'WIO', 'NWC'), feature_group_count=u.shape[-1])


def mlstm_chunkwise(q, k, v, i_pre, logf):
    B, H, S, dk = q.shape
    dv = v.shape[-1]
    L = A_CHUNK
    NC = S // L

    def to_chunks(a):
        return jnp.moveaxis(a.reshape(B, H, NC, L, *a.shape[3:]), 2, 0)

    qc, kc, vc, ic, fc = (to_chunks(a) for a in (q, k, v, i_pre, logf))
    causal = jnp.tril(jnp.ones((L, L), dtype=bool))

    def step(carry, inp):
        C, n, m = carry
        qb, kb, vb, ib, fb = inp
        b = jnp.cumsum(fb, axis=-1)
        g = b[..., -1]
        dmat = jnp.where(causal, b[..., :, None] - b[..., None, :] + ib[..., None, :], -jnp.inf)
        inter = b + m[..., None]
        m_row = jnp.maximum(inter, dmat.max(-1))
        w_intra = jnp.exp(dmat - m_row[..., None])
        w_inter = jnp.exp(inter - m_row)
        s = jnp.einsum('bhtd,bhsd->bhts', qb, kb) * w_intra
        num = (w_inter[..., None] * jnp.einsum('bhtd,bhde->bhte', qb, C)
               + jnp.einsum('bhts,bhse->bhte', s, vb))
        den = w_inter * jnp.einsum('bhtd,bhd->bht', qb, n) + s.sum(-1)
        h = num / jnp.maximum(jnp.abs(den), jnp.exp(-m_row))[..., None]
        a = g[..., None] - b + ib
        m_new = jnp.maximum(g + m, a.max(-1))
        decay = jnp.exp(g + m - m_new)
        wk = jnp.exp(a - m_new[..., None])[..., None] * kb
        C_new = decay[..., None, None] * C + jnp.einsum('bhsd,bhse->bhde', wk, vb)
        n_new = decay[..., None] * n + wk.sum(-2)
        return (C_new, n_new, m_new), h

    init = (jnp.zeros((B, H, dk, dv), jnp.float32), jnp.zeros((B, H, dk), jnp.float32),
            jnp.zeros((B, H), jnp.float32))
    _, hc = lax.scan(step, init, (qc, kc, vc, ic, fc))
    return jnp.moveaxis(hc, 0, 2).reshape(B, H, S, dv)


def multiscale_pool(u, w_pool, pool_scale):
    B, S, _ = u.shape
    uf = u.astype(jnp.float32)
    cs0 = jnp.pad(jnp.cumsum(uf, axis=1), ((0, 0), (1, 0), (0, 0)))
    t = jnp.arange(S)
    groups = []
    for gi, w in enumerate(B_WINDOWS):
        c = cs0[:, :, gi * B_GROUP_DIM:(gi + 1) * B_GROUP_DIM]
        win = c[:, 1:] - jnp.take(c, jnp.maximum(t + 1 - w, 0), axis=1)
        cnt = jnp.minimum(t + 1, w).astype(jnp.float32)[None, :, None]
        groups.append(win / cnt - uf[:, :, gi * B_GROUP_DIM:(gi + 1) * B_GROUP_DIM])
    pooled = jnp.stack(groups, axis=2)
    y = jnp.einsum('bsgc,gcd->bsgd', pooled, w_pool.astype(jnp.float32)).reshape(B, S, BW)
    return (y * pool_scale).astype(u.dtype)


def forgetting_attention(q, k, v, logf):
    B, S, H, dh = q.shape
    cT = jnp.cumsum(logf, axis=1).transpose(0, 2, 1)
    scale = dh ** -0.5
    outs = []
    for i in range(S // C_BLOCK):
        q0, q1 = i * C_BLOCK, (i + 1) * C_BLOCK
        logits = jnp.einsum('bqhd,bkhd->bhqk', q[:, q0:q1], k[:, :q1],
                            preferred_element_type=jnp.float32) * scale
        bias = cT[:, :, q0:q1, None] - cT[:, :, None, :q1]
        mask = (q0 + jnp.arange(C_BLOCK))[:, None] >= jnp.arange(q1)[None, :]
        p = jax.nn.softmax(jnp.where(mask, logits + bias, -jnp.inf), axis=-1)
        outs.append(jnp.einsum('bhqk,bkhd->bqhd', p.astype(v.dtype), v[:, :q1]))
    return jnp.concatenate(outs, axis=1)


def hybrid_mixer(x, w_in, conv_a, b_if_a, norm_a, w_pool, pool_scale, b_f_c, w_branch, w_out):
    B, S, _ = x.shape
    f32 = jnp.float32
    proj = jnp.einsum('bsd,dn->bsn', x, w_in)

    qk = jax.nn.silu(causal_dwconv(proj[..., OFF_A:OFF_A + 2 * BW], conv_a))
    va = proj[..., OFF_A + 2 * BW:OFF_A + 3 * BW]
    oa = proj[..., OFF_A + 3 * BW:OFF_A + 4 * BW]
    gif = proj[..., OFF_A + 4 * BW:OFF_A + N_A].astype(f32) + b_if_a
    i_pre = gif[..., :A_HEADS].transpose(0, 2, 1)
    logf_a = jax.nn.log_sigmoid(gif[..., A_HEADS:]).transpose(0, 2, 1)

    def heads(t):
        return t.reshape(B, S, A_HEADS, A_HEAD_DIM).transpose(0, 2, 1, 3).astype(f32)

    h_a = mlstm_chunkwise(heads(qk[..., :BW]), heads(qk[..., BW:]) * (A_HEAD_DIM ** -0.5),
                          heads(va), i_pre, logf_a)
    mu = h_a.mean(-1, keepdims=True)
    var = jnp.square(h_a - mu).mean(-1, keepdims=True)
    h_a = ((h_a - mu) * lax.rsqrt(var + LN_EPS)).transpose(0, 2, 1, 3).reshape(B, S, BW)
    h_a = (h_a * norm_a * jax.nn.sigmoid(oa.astype(f32))).astype(x.dtype)

    h_b = multiscale_pool(proj[..., OFF_B:OFF_B + N_B], w_pool, pool_scale)

    def cheads(t):
        return t.reshape(B, S, C_HEADS, C_HEAD_DIM)

    qc = cheads(proj[..., OFF_C:OFF_C + BW])
    kc = cheads(proj[..., OFF_C + BW:OFF_C + 2 * BW])
    vc = cheads(proj[..., OFF_C + 2 * BW:OFF_C + 3 * BW])
    logf_c = jax.nn.log_sigmoid(proj[..., OFF_C + 3 * BW:OFF_C + N_C].astype(f32) + b_f_c)
    h_c = forgetting_attention(qc, kc, vc, logf_c).reshape(B, S, BW)

    gates = jax.nn.sigmoid(proj[..., OFF_G:OFF_G + N_G]).reshape(B, S, N_BRANCH, D_MODEL)
    merged = gates[:, :, 0] * jnp.einsum('bsc,cd->bsd', h_a, w_branch[0])
    merged = merged + gates[:, :, 1] * jnp.einsum('bsc,cd->bsd', h_b, w_branch[1])
    merged = merged + gates[:, :, 2] * jnp.einsum('bsc,cd->bsd', h_c, w_branch[2])
    return jnp.einsum('bsd,de->bse', merged, w_out)


def grouped_moe(x, w_router, b_router, w13, w2):
    B, S, D = x.shape
    T = B * S
    xt = x.reshape(T, D)
    probs = jax.nn.softmax(jnp.einsum('td,de->te', xt, w_router).astype(jnp.float32), axis=-1)
    sel = (probs + b_router).reshape(T, N_GROUPS, EXPERTS_PER_GROUP)
    grp_score = lax.top_k(sel, TOP_K)[0].sum(-1)
    g_idx = jnp.argmax(grp_score, axis=-1)
    in_grp = jnp.take_along_axis(sel, g_idx[:, None, None], axis=1)[:, 0]
    _, local = lax.top_k(in_grp, TOP_K)
    expert_idx = g_idx[:, None] * EXPERTS_PER_GROUP + local
    wts = jnp.take_along_axis(probs, expert_idx, axis=-1)
    wts = wts / wts.sum(-1, keepdims=True)
    flat_e = expert_idx.reshape(-1)
    order = jnp.argsort(flat_e)
    tok = order // TOP_K
    group_sizes = jnp.bincount(flat_e, length=N_EXPERTS).astype(jnp.int32)
    h = lax.ragged_dot(xt[tok], w13, group_sizes)
    h = jax.nn.silu(h[:, :D_FF]) * h[:, D_FF:]
    o = lax.ragged_dot(h, w2, group_sizes)
    o = o.astype(jnp.float32) * wts.reshape(-1)[order][:, None]
    y = jax.ops.segment_sum(o, tok, num_segments=T)
    return y.reshape(B, S, D).astype(x.dtype)


def setup_inputs(seed: int = 0) -> dict:
    key = jax.random.key(seed)
    ks = jax.random.split(key, 24)
    f32 = jnp.float32

    def nrm(k, shape, s):
        return jax.random.normal(k, shape, f32) * s

    x = nrm(ks[0], (BATCH, SEQ, D_MODEL), 1.0)
    ln0_g = 1.0 + nrm(ks[1], (D_MODEL,), 0.02)
    ln0_b = nrm(ks[2], (D_MODEL,), 0.02)
    w_in = nrm(ks[3], (DEPTH, D_MODEL, IN_WIDTH), D_MODEL ** -0.5)
    conv_a = nrm(ks[4], (DEPTH, A_CONV, 2 * BW), A_CONV ** -0.5)
    b_i = nrm(ks[5], (DEPTH, A_HEADS), 0.1)
    b_f = jnp.linspace(3.0, 6.0, A_HEADS, dtype=f32)[None] + nrm(ks[6], (DEPTH, A_HEADS), 0.1)
    b_if_a = jnp.concatenate([b_i, b_f], axis=-1)
    norm_a = 1.0 + nrm(ks[7], (DEPTH, BW), 0.02)
    w_pool = nrm(ks[8], (DEPTH, B_GROUPS, B_GROUP_DIM, B_GROUP_DIM), B_GROUP_DIM ** -0.5)
    pool_scale = 1.0 + nrm(ks[9], (DEPTH, BW), 0.02)
    b_f_c = jnp.linspace(1.0, 4.0, C_HEADS, dtype=f32)[None] + nrm(ks[10], (DEPTH, C_HEADS), 0.1)
    w_branch = nrm(ks[11], (DEPTH, N_BRANCH, BW, D_MODEL), BW ** -0.5 * BETA)
    w_out = nrm(ks[12], (DEPTH, D_MODEL, D_MODEL), D_MODEL ** -0.5 * BETA)
    ln1_g = 1.0 + nrm(ks[13], (DEPTH, D_MODEL), 0.02)
    ln1_b = nrm(ks[14], (DEPTH, D_MODEL), 0.02)
    w_router = nrm(ks[15], (D_MODEL, N_EXPERTS), D_MODEL ** -0.5)
    b_router = nrm(ks[16], (N_EXPERTS,), 0.01)
    w13 = nrm(ks[17], (DEPTH, N_EXPERTS, D_MODEL, 2 * D_FF), D_MODEL ** -0.5)
    w2 = nrm(ks[18], (DEPTH, N_EXPERTS, D_FF, D_MODEL), D_FF ** -0.5 * BETA)
    ln2_g = 1.0 + nrm(ks[19], (DEPTH, D_MODEL), 0.02)
    ln2_b = nrm(ks[20], (DEPTH, D_MODEL), 0.02)
    return {"x": x, "ln0_g": ln0_g, "ln0_b": ln0_b, "w_in": w_in, "conv_a": conv_a,
            "b_if_a": b_if_a, "norm_a": norm_a, "w_pool": w_pool, "pool_scale": pool_scale,
            "b_f_c": b_f_c, "w_branch": w_branch, "w_out": w_out, "ln1_g": ln1_g,
            "ln1_b": ln1_b, "w_router": w_router, "b_router": b_router, "w13": w13,
            "w2": w2, "ln2_g": ln2_g, "ln2_b": ln2_b}


def reference(x, ln0_g, ln0_b, w_in, conv_a, b_if_a, norm_a, w_pool, pool_scale, b_f_c,
              w_branch, w_out, ln1_g, ln1_b, w_router, b_router, w13, w2, ln2_g, ln2_b):
    x = layer_norm(x, ln0_g, ln0_b)
    for l in range(DEPTH):
        h = hybrid_mixer(x, w_in[l], conv_a[l], b_if_a[l], norm_a[l], w_pool[l],
                         pool_scale[l], b_f_c[l], w_branch[l], w_out[l])
        x = layer_norm(ALPHA * x + h, ln1_g[l], ln1_b[l])
        h = grouped_moe(x, w_router, b_router, w13[l], w2[l])
        x = layer_norm(ALPHA * x + h, ln2_g[l], ln2_b[l])
    return x
```

```python
import functools

import jax
import jax.numpy as jnp
from jax import lax
from jax.experimental import pallas as pl
from jax.experimental.pallas import tpu as pltpu

F32 = jnp.float32
BF16 = jnp.bfloat16

D_MODEL = 2048
DEPTH = 2
BW = D_MODEL // 2
A_HEADS = 4
A_HEAD_DIM = BW // A_HEADS
A_CONV = 4
A_CHUNK = 128
B_WINDOWS = (2, 4, 8, 16)
B_GROUP_DIM = BW // len(B_WINDOWS)
C_HEADS = 8
C_HEAD_DIM = BW // C_HEADS
N_BRANCH = 3
OFF_A = 0
N_A = 4 * BW + 2 * A_HEADS
OFF_B = OFF_A + N_A
OFF_C = OFF_B + BW
N_C = 3 * BW + C_HEADS
OFF_G = OFF_C + N_C
N_G = N_BRANCH * D_MODEL
N_EXPERTS = 32
N_GROUPS = 8
EXPERTS_PER_GROUP = N_EXPERTS // N_GROUPS
D_FF = D_MODEL * 3 // 8
ALPHA = (2 * DEPTH) ** 0.25
LN_EPS = 1e-5

N_MAIN = 4 * BW + BW + 3 * BW + N_G
GATE_LANES = 128
N_PAIRS = 6
N_CLASSES = N_GROUPS * N_PAIRS

LANES = 128
SUBLANES = 8
VMEM_LIMIT_BYTES = 56 * 1024 * 1024

IN_TN = 1024
EPI_RB = 256
EPI_CB = 256
MERGE_TM = 256
ROUTER_TM = 1024
MOE_TM = 256
FOX_TQ = 256
LN0_TM = 512


def _params(sem, vmem=VMEM_LIMIT_BYTES):
    return pltpu.CompilerParams(dimension_semantics=sem, vmem_limit_bytes=vmem)


def _ln_rows(y, g, b):
    mu = jnp.mean(y, axis=-1, keepdims=True)
    d = y - mu
    var = jnp.mean(d * d, axis=-1, keepdims=True)
    return d * lax.rsqrt(var + LN_EPS) * g + b


def _dot(a, b):
    return jnp.dot(a, b, preferred_element_type=F32)


def _dot_nt(a, b):
    return lax.dot_general(a, b, (((1,), (1,)), ((), ())), preferred_element_type=F32)


def _dot_tn(a, b):
    return lax.dot_general(a, b, (((0,), (0,)), ((), ())), preferred_element_type=F32)


def _ln0_kernel(x_ref, g_ref, b_ref, of_ref, ob_ref):
    y = _ln_rows(x_ref[...], g_ref[...], b_ref[...])
    of_ref[...] = y
    ob_ref[...] = y.astype(BF16)


def _ln0(x2d, g, b):
    T, D = x2d.shape
    tm = LN0_TM
    row = pl.BlockSpec((tm, D), lambda i: (i, 0))
    vec = pl.BlockSpec((1, D), lambda i: (0, 0))
    return pl.pallas_call(
        _ln0_kernel,
        grid=(T // tm,),
        in_specs=[row, vec, vec],
        out_specs=[row, row],
        out_shape=[jax.ShapeDtypeStruct((T, D), F32), jax.ShapeDtypeStruct((T, D), BF16)],
        compiler_params=_params(("parallel",)),
        name="ln0",
    )(x2d, g.reshape(1, D), b.reshape(1, D))


def _inproj_kernel(x_ref, w_ref, cw_ref, o_ref, acc_ref, *, segments, tm, tn):
    j = pl.program_id(1)
    for lo, hi, kind in segments:

        @pl.when((j >= lo) & (j < hi))
        def _():
            if kind == "none":
                o_ref[...] = _dot(x_ref[...], w_ref[...]).astype(o_ref.dtype)
            elif kind == "sigmoid":
                o_ref[...] = jax.nn.sigmoid(_dot(x_ref[...], w_ref[...])).astype(o_ref.dtype)
            else:
                acc_ref[0:SUBLANES, :] = jnp.zeros((SUBLANES, tn), F32)
                acc_ref[SUBLANES:, :] = _dot(x_ref[...], w_ref[...])
                rb_n = min(EPI_RB, tm)
                for rb in range(tm // rb_n):
                    r0 = rb * rb_n
                    for cb in range(tn // EPI_CB):
                        cs = slice(cb * EPI_CB, (cb + 1) * EPI_CB)
                        y = cw_ref[A_CONV - 1:A_CONV, cs] * acc_ref[r0 + SUBLANES:r0 + SUBLANES + rb_n, cs]
                        for d in range(1, A_CONV):
                            tap = cw_ref[A_CONV - 1 - d:A_CONV - d, cs]
                            y = y + tap * acc_ref[r0 + SUBLANES - d:r0 + SUBLANES - d + rb_n, cs]
                        y = y * jax.nn.sigmoid(y) * cw_ref[A_CONV:A_CONV + 1, cs]
                        o_ref[r0:r0 + rb_n, cs] = y.astype(o_ref.dtype)


def _inproj(xb, w_main, cw, S):
    T, K = xb.shape
    N = w_main.shape[1]
    tm, tn = S, IN_TN
    t = lambda c: c // tn
    segments = (
        (t(0), t(2 * BW), "conv"),
        (t(2 * BW), t(3 * BW), "none"),
        (t(3 * BW), t(4 * BW), "sigmoid"),
        (t(4 * BW), t(8 * BW), "none"),
        (t(8 * BW), t(N), "sigmoid"),
    )
    return pl.pallas_call(
        functools.partial(_inproj_kernel, segments=segments, tm=tm, tn=tn),
        grid=(T // tm, N // tn),
        in_specs=[
            pl.BlockSpec((tm, K), lambda i, j: (i, 0)),
            pl.BlockSpec((K, tn), lambda i, j: (0, j)),
            pl.BlockSpec((SUBLANES, tn), lambda i, j: (0, j)),
        ],
        out_specs=pl.BlockSpec((tm, tn), lambda i, j: (i, j)),
        out_shape=jax.ShapeDtypeStruct((T, N), BF16),
        scratch_shapes=[pltpu.VMEM((tm + SUBLANES, tn), F32)],
        compiler_params=_params(("parallel", "arbitrary")),
        name="inproj",
    )(xb, w_main, cw)


def _gates_kernel(x_ref, wg_ref, bias_ref, gcol_ref, grow_ref, *, S):
    g = _dot(x_ref[...], wg_ref[...]) + bias_ref[...]
    col = lax.broadcasted_iota(jnp.int32, (1, GATE_LANES), 1)
    n_a = 2 * A_HEADS
    logf = jnp.where((col >= A_HEADS) & (col < n_a + C_HEADS), jax.nn.log_sigmoid(g), 0.0)
    nch = S // A_CHUNK
    cat = jnp.concatenate([logf[c * A_CHUNK:(c + 1) * A_CHUNK, :] for c in range(nch)], axis=1)
    ri = lax.broadcasted_iota(jnp.int32, (A_CHUNK, A_CHUNK), 0)
    ci = lax.broadcasted_iota(jnp.int32, (A_CHUNK, A_CHUNK), 1)
    tri = jnp.where(ri >= ci, 1.0, 0.0).astype(BF16)
    hi = cat.astype(BF16)
    r1 = cat - hi.astype(F32)
    mid = r1.astype(BF16)
    low = (r1 - mid.astype(F32)).astype(BF16)
    within = _dot(tri, hi) + _dot(tri, mid) + _dot(tri, low)
    carry = jnp.zeros((1, GATE_LANES), F32)
    for c in range(nch):
        rows = slice(c * A_CHUNK, (c + 1) * A_CHUNK)
        wc = within[:, c * GATE_LANES:(c + 1) * GATE_LANES]
        glob = wc + carry
        carry = carry + wc[A_CHUNK - 1:A_CHUNK, :]
        gcol_ref[rows, :] = jnp.where(col < A_HEADS, g[rows, :], jnp.where(col < n_a, wc, glob))
    grow_ref[0] = gcol_ref[...].T[0:2 * SUBLANES, :]


def _gates(xb, wg, bias, B, S):
    T, K = xb.shape
    return pl.pallas_call(
        functools.partial(_gates_kernel, S=S),
        grid=(B,),
        in_specs=[
            pl.BlockSpec((S, K), lambda b: (b, 0)),
            pl.BlockSpec((K, GATE_LANES), lambda b: (0, 0)),
            pl.BlockSpec((1, GATE_LANES), lambda b: (0, 0)),
        ],
        out_specs=[
            pl.BlockSpec((S, GATE_LANES), lambda b: (b, 0)),
            pl.BlockSpec((1, 2 * SUBLANES, S), lambda b: (b, 0, 0)),
        ],
        out_shape=[
            jax.ShapeDtypeStruct((T, GATE_LANES), F32),
            jax.ShapeDtypeStruct((B, 2 * SUBLANES, S), F32),
        ],
        compiler_params=_params(("parallel",)),
        name="gates",
    )(xb, wg, bias)


def _mlstm_kernel(q_ref, k_ref, v_ref, o_ref, gcol_ref, grow_ref, na_ref, out_ref,
                  c_ref, n_ref, m_ref):
    L = A_CHUNK
    dh = A_HEAD_DIM

    @pl.when(pl.program_id(1) == 0)
    def _():
        c_ref[...] = jnp.zeros_like(c_ref)
        n_ref[...] = jnp.zeros_like(n_ref)
        m_ref[...] = jnp.zeros_like(m_ref)

    ti = lax.broadcasted_iota(jnp.int32, (L, L), 0)
    si = lax.broadcasted_iota(jnp.int32, (L, L), 1)
    causal = si <= ti
    gcol = gcol_ref[...]
    grow = grow_ref[0]
    for h in range(A_HEADS):
        cs = slice(h * dh, (h + 1) * dh)
        q = q_ref[:, cs]
        k = k_ref[:, cs]
        v = v_ref[:, cs]
        i_col = gcol[:, h:h + 1]
        b_col = gcol[:, A_HEADS + h:A_HEADS + h + 1]
        i_row = grow[h:h + 1, :]
        b_row = grow[A_HEADS + h:A_HEADS + h + 1, :]
        g = b_col[L - 1:L, :]
        m = m_ref[h]
        dmat = jnp.where(causal, b_col - b_row + i_row, -jnp.inf)
        inter = b_col + m
        m_row = jnp.maximum(inter, jnp.max(dmat, axis=1, keepdims=True))
        w_intra = jnp.exp(dmat - m_row)
        w_inter = jnp.exp(inter - m_row)
        s = _dot_nt(q, k) * w_intra
        c_state = c_ref[h]
        num = w_inter * _dot(q, c_state.astype(BF16)) + _dot(s.astype(BF16), v)
        qn = jnp.sum(q.astype(F32) * n_ref[h], axis=1, keepdims=True)
        den = w_inter * qn + jnp.sum(s, axis=1, keepdims=True)
        hh = num / jnp.maximum(jnp.abs(den), jnp.exp(-m_row))
        a_col = g - b_col + i_col
        m_new = jnp.maximum(g + m, jnp.max(a_col, axis=0, keepdims=True))
        decay = jnp.exp(g + m - m_new)
        wk = jnp.exp(a_col - m_new) * k.astype(F32)
        c_ref[h] = decay * c_state + _dot_tn(wk.astype(BF16), v)
        n_ref[h] = decay * n_ref[h] + jnp.sum(wk, axis=0, keepdims=True)
        m_ref[h] = m_new
        mu = jnp.mean(hh, axis=1, keepdims=True)
        d = hh - mu
        var = jnp.mean(d * d, axis=1, keepdims=True)
        hn = d * lax.rsqrt(var + LN_EPS) * na_ref[:, cs] * o_ref[:, cs].astype(F32)
        out_ref[:, cs] = hn.astype(out_ref.dtype)


def _mlstm(proj, gcol, grow, norm_a, B, S):
    T = proj.shape[0]
    L = A_CHUNK
    nc = S // L
    blk = lambda cb: pl.BlockSpec((L, BW), lambda b, c, cb=cb: (b * nc + c, cb))
    return pl.pallas_call(
        _mlstm_kernel,
        grid=(B, nc),
        in_specs=[
            blk(0), blk(1), blk(2), blk(3),
            pl.BlockSpec((L, GATE_LANES), lambda b, c: (b * nc + c, 0)),
            pl.BlockSpec((1, 2 * SUBLANES, L), lambda b, c: (b, 0, c)),
            pl.BlockSpec((1, BW), lambda b, c: (0, 0)),
        ],
        out_specs=pl.BlockSpec((L, BW), lambda b, c: (b * nc + c, 0)),
        out_shape=jax.ShapeDtypeStruct((T, BW), BF16),
        scratch_shapes=[
            pltpu.VMEM((A_HEADS, A_HEAD_DIM, A_HEAD_DIM), F32),
            pltpu.VMEM((A_HEADS, 1, A_HEAD_DIM), F32),
            pltpu.VMEM((A_HEADS, 1, 1), F32),
        ],
        compiler_params=_params(("parallel", "arbitrary")),
        name="mlstm",
    )(proj, proj, proj, proj, gcol, grow, norm_a.reshape(1, BW))


def _pool_kernel(u_ref, wp_ref, ps_ref, out_ref, *, S):
    t = lax.broadcasted_iota(jnp.int32, (S, 1), 0)
    gd = B_GROUP_DIM
    for gi, w in enumerate(B_WINDOWS):
        cs = slice(gi * gd, (gi + 1) * gd)
        u = u_ref[:, cs].astype(F32)
        s = u
        k = 1
        while k < w:
            s = s + jnp.where(t >= k, pltpu.roll(s, k, axis=0), 0.0)
            k *= 2
        cnt = jnp.minimum(t + 1, w).astype(F32)
        pooled = s / cnt - u
        y = _dot(pooled.astype(BF16), wp_ref[gi]) * ps_ref[:, cs]
        out_ref[:, cs] = y.astype(out_ref.dtype)


def _pool(proj, w_pool, pool_scale, B, S):
    T = proj.shape[0]
    G = len(B_WINDOWS)
    return pl.pallas_call(
        functools.partial(_pool_kernel, S=S),
        grid=(B,),
        in_specs=[
            pl.BlockSpec((S, BW), lambda b: (b, 4)),
            pl.BlockSpec((G, B_GROUP_DIM, B_GROUP_DIM), lambda b: (0, 0, 0)),
            pl.BlockSpec((1, BW), lambda b: (0, 0)),
        ],
        out_specs=pl.BlockSpec((S, BW), lambda b: (b, 0)),
        out_shape=jax.ShapeDtypeStruct((T, BW), BF16),
        compiler_params=_params(("parallel",)),
        name="pool",
    )(proj, w_pool, pool_scale.reshape(1, BW))


def _fox_kernel(q_ref, k_ref, v_ref, gcol_ref, grow_ref, out_ref, *, tq):
    qi = pl.program_id(1)
    dh = C_HEAD_DIM
    scale = dh ** -0.5
    gcol = gcol_ref[...]
    ri = lax.broadcasted_iota(jnp.int32, (tq, tq), 0)
    ci = lax.broadcasted_iota(jnp.int32, (tq, tq), 1)
    diag = ci <= ri
    n_a = 2 * A_HEADS
    for h in range(C_HEADS):
        cs = slice(h * dh, (h + 1) * dh)
        q = q_ref[:, cs]
        c_col = gcol[:, n_a + h:n_a + h + 1]

        def step(kj, carry, masked, cs=cs, q=q, c_col=c_col, h=h):
            m, l, acc = carry
            k0 = pl.multiple_of(kj * tq, tq)
            k = k_ref[pl.ds(k0, tq), cs]
            v = v_ref[pl.ds(k0, tq), cs]
            c_row = grow_ref[0, n_a + h:n_a + h + 1, pl.ds(k0, tq)]
            lg = _dot_nt(q, k) * scale + (c_col - c_row)
            if masked:
                lg = jnp.where(diag, lg, -jnp.inf)
            m_new = jnp.maximum(m, jnp.max(lg, axis=1, keepdims=True))
            p = jnp.exp(lg - m_new)
            alpha = jnp.exp(m - m_new)
            l = alpha * l + jnp.sum(p, axis=1, keepdims=True)
            acc = alpha * acc + _dot(p.astype(BF16), v)
            return m_new, l, acc

        init = (jnp.full((tq, 1), -jnp.inf, F32), jnp.zeros((tq, 1), F32), jnp.zeros((tq, dh), F32))
        carry = lax.fori_loop(0, qi, lambda kj, c: step(kj, c, False), init)
        m, l, acc = step(qi, carry, True)
        out_ref[:, cs] = (acc / l).astype(out_ref.dtype)


def _fox(proj, gcol, grow, B, S):
    T = proj.shape[0]
    tq = min(FOX_TQ, S)
    nq = S // tq
    return pl.pallas_call(
        functools.partial(_fox_kernel, tq=tq),
        grid=(B, nq),
        in_specs=[
            pl.BlockSpec((tq, BW), lambda b, i: (b * nq + i, 5)),
            pl.BlockSpec((S, BW), lambda b, i: (b, 6)),
            pl.BlockSpec((S, BW), lambda b, i: (b, 7)),
            pl.BlockSpec((tq, GATE_LANES), lambda b, i: (b * nq + i, 0)),
            pl.BlockSpec((1, 2 * SUBLANES, S), lambda b, i: (b, 0, 0)),
        ],
        out_specs=pl.BlockSpec((tq, BW), lambda b, i: (b * nq + i, 0)),
        out_shape=jax.ShapeDtypeStruct((T, BW), BF16),
        compiler_params=_params(("parallel", "arbitrary")),
        name="fox",
    )(proj, proj, proj, gcol, grow)


def _merge_kernel(ha_ref, hb_ref, hc_ref, g0_ref, g1_ref, g2_ref, wb_ref, wo_ref, x_ref,
                  lg_ref, lb_ref, xo_ref):
    merged = g0_ref[...].astype(F32) * _dot(ha_ref[...], wb_ref[0])
    merged = merged + g1_ref[...].astype(F32) * _dot(hb_ref[...], wb_ref[1])
    merged = merged + g2_ref[...].astype(F32) * _dot(hc_ref[...], wb_ref[2])
    h = _dot(merged.astype(BF16), wo_ref[...])
    xo_ref[...] = _ln_rows(ALPHA * x_ref[...] + h, lg_ref[...], lb_ref[...])


def _merge(ha, hb, hc, proj, w_branch, w_out, x, ln_g, ln_b):
    T, D = x.shape
    tm = MERGE_TM
    hblk = pl.BlockSpec((tm, BW), lambda i: (i, 0))
    gblk = lambda n: pl.BlockSpec((tm, D), lambda i, n=n: (i, 4 + n))
    const = lambda shape: pl.BlockSpec(shape, lambda i: (0,) * len(shape), pipeline_mode=pl.Buffered(1))
    row = pl.BlockSpec((tm, D), lambda i: (i, 0))
    return pl.pallas_call(
        _merge_kernel,
        grid=(T // tm,),
        in_specs=[hblk, hblk, hblk, gblk(0), gblk(1), gblk(2),
                  const((N_BRANCH, BW, D)), const((D, D)), row, const((1, D)), const((1, D))],
        out_specs=row,
        out_shape=jax.ShapeDtypeStruct((T, D), F32),
        compiler_params=_params(("parallel",)),
        name="merge",
    )(ha, hb, hc, proj, proj, proj, w_branch, w_out, x, ln_g.reshape(1, D), ln_b.reshape(1, D))


def _router_kernel(x_ref, wr_ref, br_ref, oi_ref, ow_ref):
    tm = x_ref.shape[0]
    ng, ne = N_GROUPS, EXPERTS_PER_GROUP
    lt = lax.dot_general(wr_ref[...], x_ref[...], (((1,), (1,)), ((), ())),
                         precision=lax.Precision.HIGHEST, preferred_element_type=F32)
    logit = [lt[ng * j:ng * (j + 1), :] for j in range(ne)]
    mx = functools.reduce(jnp.maximum, logit)
    mx = jnp.max(mx, axis=0, keepdims=True)
    ex = [jnp.exp(a - mx) for a in logit]
    den = jnp.sum(functools.reduce(jnp.add, ex), axis=0, keepdims=True)
    prob = [e / den for e in ex]
    sel = [prob[j] + br_ref[ng * j:ng * (j + 1), :] for j in range(ne)]

    def first_argmax(vals):
        best = functools.reduce(jnp.maximum, vals)
        idx = jnp.full(best.shape, float(ne - 1), F32)
        for j in range(ne - 2, -1, -1):
            idx = jnp.where(vals[j] == best, float(j), idx)
        return best, idx

    def pick(vals, idx):
        out = vals[ne - 1]
        for j in range(ne - 2, -1, -1):
            out = jnp.where(idx == float(j), vals[j], out)
        return out

    m1, i1 = first_argmax(sel)
    rest = [jnp.where(i1 == float(j), -jnp.inf, sel[j]) for j in range(ne)]
    m2, i2 = first_argmax(rest)
    score = m1 + m2
    gi = lax.broadcasted_iota(jnp.int32, (ng, tm), 0).astype(F32)
    gidx = jnp.min(jnp.where(score == jnp.max(score, axis=0, keepdims=True), gi, float(ng)),
                   axis=0, keepdims=True)
    chosen = gi == gidx
    take = lambda a: jnp.sum(jnp.where(chosen, a, 0.0), axis=0, keepdims=True)
    e1, e2 = take(i1), take(i2)
    p1, p2 = take(pick(prob, i1)), take(pick(prob, i2))
    psum = p1 + p2
    p1, p2 = p1 / psum, p2 / psum
    lo, hi = jnp.minimum(e1, e2), jnp.maximum(e1, e2)
    pair = jnp.where(lo == 0.0, hi - 1.0, jnp.where(lo == 1.0, hi + 1.0, float(N_PAIRS - 1)))
    w_lo = jnp.where(e1 < e2, p1, p2)
    w_hi = jnp.where(e1 < e2, p2, p1)
    zero = jnp.zeros((1, tm), F32)
    rows_i = [gidx * N_PAIRS + pair, gidx * ne + lo, gidx * ne + hi] + [zero] * (SUBLANES - 3)
    rows_w = [w_lo, w_hi] + [zero] * (SUBLANES - 2)
    oi_ref[...] = jnp.concatenate(rows_i, axis=0).astype(jnp.int32)
    ow_ref[...] = jnp.concatenate(rows_w, axis=0)


def _router(x, wr_t, br_t):
    T, D = x.shape
    tm = min(ROUTER_TM, T)
    return pl.pallas_call(
        _router_kernel,
        grid=(T // tm,),
        in_specs=[
            pl.BlockSpec((tm, D), lambda i: (i, 0)),
            pl.BlockSpec((N_EXPERTS, D), lambda i: (0, 0)),
            pl.BlockSpec((N_EXPERTS, 1), lambda i: (0, 0)),
        ],
        out_specs=[pl.BlockSpec((SUBLANES, tm), lambda i: (0, i))] * 2,
        out_shape=[jax.ShapeDtypeStruct((SUBLANES, T), jnp.int32),
                   jax.ShapeDtypeStruct((SUBLANES, T), F32)],
        compiler_params=_params(("parallel",)),
        name="router",
    )(x, wr_t, br_t)


def _moe_kernel(e1_ref, e2_ref, valid_ref, src_ref, wts_ref, x_hbm, w13a_ref, w13b_ref,
                w2a_ref, w2b_ref, lg_ref, lb_ref, out_hbm, xbuf, obuf, sem_in, sem_out, *, tm):
    i = pl.program_id(0)

    def in_copy(r, tok):
        return pltpu.make_async_copy(x_hbm.at[pl.ds(tok, 1)], xbuf.at[pl.ds(r, 1)], sem_in)

    def out_copy(r, tok):
        return pltpu.make_async_copy(obuf.at[pl.ds(r, 1)], out_hbm.at[pl.ds(tok, 1)], sem_out)

    @pl.when(valid_ref[i] == 1)
    def _():
        def start_in(r, c):
            in_copy(r, jnp.maximum(src_ref[0, 0, r], 0)).start()
            return c

        def wait_in(r, c):
            in_copy(r, 0).wait()
            return c

        lax.fori_loop(0, tm, start_in, 0)
        lax.fori_loop(0, tm, wait_in, 0)

        x = xbuf[...]
        xb = x.astype(BF16)

        def expert(w13_ref, w2_ref):
            h = _dot(xb, w13_ref[0])
            a = jax.nn.silu(h[:, :D_FF]) * h[:, D_FF:]
            return _dot(a.astype(BF16), w2_ref[0])

        y = wts_ref[:, 0:1] * expert(w13a_ref, w2a_ref)
        y = y + wts_ref[:, 1:2] * expert(w13b_ref, w2b_ref)
        obuf[...] = _ln_rows(ALPHA * x + y, lg_ref[...], lb_ref[...])

        def start_out(r, c):
            tok = src_ref[0, 0, r]

            @pl.when(tok >= 0)
            def _():
                out_copy(r, tok).start()

            return c

        def wait_out(r, c):
            @pl.when(src_ref[0, 0, r] >= 0)
            def _():
                out_copy(r, 0).wait()

            return c

        lax.fori_loop(0, tm, start_out, 0)
        lax.fori_loop(0, tm, wait_out, 0)


def _moe(x, oi, ow, w13, w2, ln_g, ln_b):
    T, D = x.shape
    tm = MOE_TM
    npad = T + N_CLASSES * tm
    nt = npad // tm
    cls = oi[0]
    counts = jnp.zeros((N_CLASSES,), jnp.int32).at[cls].add(1)
    ptiles = (counts + tm - 1) // tm
    tile_end = jnp.cumsum(ptiles)
    tile_off = tile_end - ptiles
    row_off = jnp.cumsum(counts) - counts
    order = jnp.argsort(cls, stable=True).astype(jnp.int32)
    cls_sorted = cls[order]
    pos = tile_off[cls_sorted] * tm + (jnp.arange(T, dtype=jnp.int32) - row_off[cls_sorted])
    src = jnp.full((npad,), -1, jnp.int32).at[pos].set(order)
    tile_id = jnp.arange(nt, dtype=jnp.int32)
    n_used = tile_end[-1]
    tile_valid = (tile_id < n_used).astype(jnp.int32)
    tile_cls = jnp.searchsorted(tile_end, jnp.minimum(tile_id, n_used - 1), side="right").astype(jnp.int32)
    pair = tile_cls % N_PAIRS
    lo_tbl = jnp.array([0, 0, 0, 1, 1, 2], jnp.int32)
    hi_tbl = jnp.array([1, 2, 3, 2, 3, 3], jnp.int32)
    tile_e1 = (tile_cls // N_PAIRS) * EXPERTS_PER_GROUP + lo_tbl[pair]
    tile_e2 = (tile_cls // N_PAIRS) * EXPERTS_PER_GROUP + hi_tbl[pair]
    wts = jnp.take(ow[0:2].T, jnp.maximum(src, 0), axis=0)

    F2 = 2 * D_FF
    grid_spec = pltpu.PrefetchScalarGridSpec(
        num_scalar_prefetch=3,
        grid=(nt,),
        in_specs=[
            pl.BlockSpec((1, 1, tm), lambda i, e1, e2, va: (i, 0, 0), memory_space=pltpu.SMEM),
            pl.BlockSpec((tm, 2), lambda i, e1, e2, va: (i, 0)),
            pl.BlockSpec(memory_space=pl.ANY),
            pl.BlockSpec((1, D, F2), lambda i, e1, e2, va: (e1[i], 0, 0)),
            pl.BlockSpec((1, D, F2), lambda i, e1, e2, va: (e2[i], 0, 0)),
            pl.BlockSpec((1, D_FF, D), lambda i, e1, e2, va: (e1[i], 0, 0)),
            pl.BlockSpec((1, D_FF, D), lambda i, e1, e2, va: (e2[i], 0, 0)),
            pl.BlockSpec((1, D), lambda i, e1, e2, va: (0, 0)),
            pl.BlockSpec((1, D), lambda i, e1, e2, va: (0, 0)),
        ],
        out_specs=pl.BlockSpec(memory_space=pl.ANY),
        scratch_shapes=[
            pltpu.VMEM((tm, D), F32),
            pltpu.VMEM((tm, D), F32),
            pltpu.SemaphoreType.DMA(()),
            pltpu.SemaphoreType.DMA(()),
        ],
    )
    return pl.pallas_call(
        functools.partial(_moe_kernel, tm=tm),
        grid_spec=grid_spec,
        out_shape=jax.ShapeDtypeStruct((T, D), F32),
        compiler_params=_params(("arbitrary",)),
        name="moe",
    )(tile_e1, tile_e2, tile_valid, src.reshape(nt, 1, tm), wts, x, w13, w13, w2, w2,
      ln_g.reshape(1, D), ln_b.reshape(1, D))


def _pack_layer_weights(w_in, conv_a, b_if_a, b_f_c):
    a_end = OFF_A + 4 * BW
    w_main = jnp.concatenate(
        [w_in[:, OFF_A:a_end], w_in[:, OFF_B:OFF_B + BW], w_in[:, OFF_C:OFF_C + 3 * BW],
         w_in[:, OFF_G:OFF_G + N_G]], axis=1).astype(BF16)
    n_gate = 2 * A_HEADS + C_HEADS
    wg = jnp.concatenate([w_in[:, a_end:a_end + 2 * A_HEADS], w_in[:, OFF_C + 3 * BW:OFF_C + N_C]], axis=1)
    wg = jnp.pad(wg, ((0, 0), (0, GATE_LANES - n_gate))).astype(BF16)
    bias = jnp.pad(jnp.concatenate([b_if_a, b_f_c]), (0, GATE_LANES - n_gate)).reshape(1, GATE_LANES)
    k_scale = jnp.concatenate([jnp.ones((BW,), F32), jnp.full((BW,), A_HEAD_DIM ** -0.5, F32)])
    cw = jnp.concatenate([conv_a, k_scale[None, :], jnp.zeros((SUBLANES - A_CONV - 1, 2 * BW), F32)], axis=0)
    cw = jnp.pad(cw, ((0, 0), (0, N_MAIN - 2 * BW)))
    return w_main, wg, bias, cw


def _forward(x, ln0_g, ln0_b, w_in, conv_a, b_if_a, norm_a, w_pool, pool_scale, b_f_c,
             w_branch, w_out, ln1_g, ln1_b, w_router, b_router, w13, w2, ln2_g, ln2_b):
    B, S, D = x.shape
    T = B * S
    xf, xb = _ln0(x.reshape(T, D), ln0_g, ln0_b)
    perm = (jnp.arange(N_EXPERTS) % N_GROUPS) * EXPERTS_PER_GROUP + jnp.arange(N_EXPERTS) // N_GROUPS
    wr_t = w_router.T[perm]
    br_t = b_router[perm].reshape(N_EXPERTS, 1)
    for l in range(w_in.shape[0]):
        w_main, wg, bias, cw = _pack_layer_weights(w_in[l], conv_a[l], b_if_a[l], b_f_c[l])
        proj = _inproj(xb, w_main, cw, S)
        gcol, grow = _gates(xb, wg, bias, B, S)
        ha = _mlstm(proj, gcol, grow, norm_a[l], B, S)
        hb = _pool(proj, w_pool[l].astype(BF16), pool_scale[l], B, S)
        hc = _fox(proj, gcol, grow, B, S)
        x1 = _merge(ha, hb, hc, proj, w_branch[l].astype(BF16), w_out[l].astype(BF16), xf,
                    ln1_g[l], ln1_b[l])
        oi, ow = _router(x1, wr_t, br_t)
        xf = _moe(x1, oi, ow, w13[l].astype(BF16), w2[l].astype(BF16), ln2_g[l], ln2_b[l])
        xb = xf.astype(BF16)
    return xf.reshape(B, S, D)


def kernel(x, ln0_g, ln0_b, w_in, conv_a, b_if_a, norm_a, w_pool, pool_scale, b_f_c, w_branch,
           w_out, ln1_g, ln1_b, w_router, b_router, w13, w2, ln2_g, ln2_b):
    return _forward(x, ln0_g, ln0_b, w_in, conv_a, b_if_a, norm_a, w_pool, pool_scale, b_f_c,
                    w_branch, w_out, ln1_g, ln1_b, w_router, b_router, w13, w2, ln2_g, ln2_b)
```

```python
import functools

import jax
import jax.numpy as jnp
from jax import lax
from jax.experimental import pallas as pl
from jax.experimental.pallas import tpu as pltpu

F32 = jnp.float32
BF16 = jnp.bfloat16

D_MODEL = 2048
DEPTH = 2
BW = D_MODEL // 2
A_HEADS = 4
A_HEAD_DIM = BW // A_HEADS
A_CONV = 4
A_CHUNK = 128
B_WINDOWS = (2, 4, 8, 16)
B_GROUP_DIM = BW // len(B_WINDOWS)
C_HEADS = 8
C_HEAD_DIM = BW // C_HEADS
N_BRANCH = 3
OFF_A = 0
N_A = 4 * BW + 2 * A_HEADS
OFF_B = OFF_A + N_A
OFF_C = OFF_B + BW
N_C = 3 * BW + C_HEADS
OFF_G = OFF_C + N_C
N_G = N_BRANCH * D_MODEL
N_EXPERTS = 32
N_GROUPS = 8
EXPERTS_PER_GROUP = N_EXPERTS // N_GROUPS
D_FF = D_MODEL * 3 // 8
ALPHA = (2 * DEPTH) ** 0.25
LN_EPS = 1e-5

N_MAIN = 4 * BW + BW + 3 * BW + N_G
GATE_LANES = 128
N_PAIRS = 6
N_CLASSES = N_GROUPS * N_PAIRS
PAIR_SLOT_A = (0, 0, 0, 1, 1, 3)
PAIR_SLOT_B = (1, 2, 3, 3, 2, 2)

LANES = 128
SUBLANES = 8
VMEM_LIMIT_BYTES = 56 * 1024 * 1024

IN_TN = 1024
IN_RM = 256
EPI_CB = 256
MERGE_TM = 256
ROUTER_TM = 1024
MOE_TM = 256
MOE_DMA_UNROLL = 8
FOX_TQ = 256
LN0_TM = 512


def _params(sem, vmem=VMEM_LIMIT_BYTES):
    return pltpu.CompilerParams(dimension_semantics=sem, vmem_limit_bytes=vmem)


def _ln_rows(y, g, b):
    mu = jnp.mean(y, axis=-1, keepdims=True)
    d = y - mu
    var = jnp.mean(d * d, axis=-1, keepdims=True)
    return d * lax.rsqrt(var + LN_EPS) * g + b


def _dot(a, b):
    return jnp.dot(a, b, preferred_element_type=F32)


def _dot_nt(a, b):
    return lax.dot_general(a, b, (((1,), (1,)), ((), ())), preferred_element_type=F32)


def _dot_tn(a, b):
    return lax.dot_general(a, b, (((0,), (0,)), ((), ())), preferred_element_type=F32)


def _ln0_kernel(x_ref, g_ref, b_ref, of_ref, ob_ref):
    y = _ln_rows(x_ref[...], g_ref[...], b_ref[...])
    of_ref[...] = y
    ob_ref[...] = y.astype(BF16)


def _ln0(x2d, g, b):
    T, D = x2d.shape
    tm = LN0_TM
    row = pl.BlockSpec((tm, D), lambda i: (i, 0))
    vec = pl.BlockSpec((1, D), lambda i: (0, 0))
    return pl.pallas_call(
        _ln0_kernel,
        grid=(T // tm,),
        in_specs=[row, vec, vec],
        out_specs=[row, row],
        out_shape=[jax.ShapeDtypeStruct((T, D), F32), jax.ShapeDtypeStruct((T, D), BF16)],
        compiler_params=_params(("parallel",)),
        name="ln0",
    )(x2d, g.reshape(1, D), b.reshape(1, D))


def _inproj_kernel(x_ref, w_ref, cw_ref, o_ref, acc_ref, *, segments, tm, tn):
    j = pl.program_id(1)
    rm = min(IN_RM, tm)

    def chunk(mi, kind):
        r0 = pl.multiple_of(mi * rm, rm)
        acc = _dot(x_ref[pl.ds(r0, rm), :], w_ref[...])
        if kind == "none":
            o_ref[pl.ds(r0, rm), :] = acc.astype(o_ref.dtype)
        elif kind == "sigmoid":
            o_ref[pl.ds(r0, rm), :] = jax.nn.sigmoid(acc).astype(o_ref.dtype)
        else:
            @pl.when(mi == 0)
            def _():
                acc_ref[0:SUBLANES, :] = jnp.zeros((SUBLANES, tn), F32)

            acc_ref[SUBLANES:, :] = acc
            for cb in range(tn // EPI_CB):
                cs = slice(cb * EPI_CB, (cb + 1) * EPI_CB)
                y = cw_ref[A_CONV - 1:A_CONV, cs] * acc_ref[SUBLANES:SUBLANES + rm, cs]
                for d in range(1, A_CONV):
                    tap = cw_ref[A_CONV - 1 - d:A_CONV - d, cs]
                    y = y + tap * acc_ref[SUBLANES - d:SUBLANES - d + rm, cs]
                y = y * jax.nn.sigmoid(y) * cw_ref[A_CONV:A_CONV + 1, cs]
                o_ref[pl.ds(r0, rm), cs] = y.astype(o_ref.dtype)
            acc_ref[0:SUBLANES, :] = acc_ref[rm:rm + SUBLANES, :]

    for lo, hi, kind in segments:

        @pl.when((j >= lo) & (j < hi))
        def _():
            def body(mi, carry):
                chunk(mi, kind)
                return carry

            lax.fori_loop(0, tm // rm, body, 0)


def _inproj(xb, w_main, cw, S):
    T, K = xb.shape
    N = w_main.shape[1]
    tm, tn = S, IN_TN
    t = lambda c: c // tn
    segments = (
        (t(0), t(2 * BW), "conv"),
        (t(2 * BW), t(3 * BW), "none"),
        (t(3 * BW), t(4 * BW), "sigmoid"),
        (t(4 * BW), t(8 * BW), "none"),
        (t(8 * BW), t(N), "sigmoid"),
    )
    return pl.pallas_call(
        functools.partial(_inproj_kernel, segments=segments, tm=tm, tn=tn),
        grid=(T // tm, N // tn),
        in_specs=[
            pl.BlockSpec((tm, K), lambda i, j: (i, 0)),
            pl.BlockSpec((K, tn), lambda i, j: (0, j)),
            pl.BlockSpec((SUBLANES, tn), lambda i, j: (0, j)),
        ],
        out_specs=pl.BlockSpec((tm, tn), lambda i, j: (i, j)),
        out_shape=jax.ShapeDtypeStruct((T, N), BF16),
        scratch_shapes=[pltpu.VMEM((min(IN_RM, tm) + SUBLANES, tn), F32)],
        compiler_params=_params(("parallel", "arbitrary")),
        name="inproj",
    )(xb, w_main, cw)


def _gates_kernel(x_ref, wg_ref, bias_ref, gcol_ref, grow_ref, *, S):
    g = _dot(x_ref[...], wg_ref[...]) + bias_ref[...]
    col = lax.broadcasted_iota(jnp.int32, (1, GATE_LANES), 1)
    n_a = 2 * A_HEADS
    logf = jnp.where((col >= A_HEADS) & (col < n_a + C_HEADS), jax.nn.log_sigmoid(g), 0.0)
    nch = S // A_CHUNK
    cat = jnp.concatenate([logf[c * A_CHUNK:(c + 1) * A_CHUNK, :] for c in range(nch)], axis=1)
    ri = lax.broadcasted_iota(jnp.int32, (A_CHUNK, A_CHUNK), 0)
    ci = lax.broadcasted_iota(jnp.int32, (A_CHUNK, A_CHUNK), 1)
    tri = jnp.where(ri >= ci, 1.0, 0.0).astype(BF16)
    hi = cat.astype(BF16)
    r1 = cat - hi.astype(F32)
    mid = r1.astype(BF16)
    low = (r1 - mid.astype(F32)).astype(BF16)
    within = _dot(tri, hi) + _dot(tri, mid) + _dot(tri, low)
    carry = jnp.zeros((1, GATE_LANES), F32)
    for c in range(nch):
        rows = slice(c * A_CHUNK, (c + 1) * A_CHUNK)
        wc = within[:, c * GATE_LANES:(c + 1) * GATE_LANES]
        glob = wc + carry
        carry = carry + wc[A_CHUNK - 1:A_CHUNK, :]
        gcol_ref[rows, :] = jnp.where(col < A_HEADS, g[rows, :], jnp.where(col < n_a, wc, glob))
    grow_ref[0] = gcol_ref[...].T[0:2 * SUBLANES, :]


def _gates(xb, wg, bias, B, S):
    T, K = xb.shape
    return pl.pallas_call(
        functools.partial(_gates_kernel, S=S),
        grid=(B,),
        in_specs=[
            pl.BlockSpec((S, K), lambda b: (b, 0)),
            pl.BlockSpec((K, GATE_LANES), lambda b: (0, 0)),
            pl.BlockSpec((1, GATE_LANES), lambda b: (0, 0)),
        ],
        out_specs=[
            pl.BlockSpec((S, GATE_LANES), lambda b: (b, 0)),
            pl.BlockSpec((1, 2 * SUBLANES, S), lambda b: (b, 0, 0)),
        ],
        out_shape=[
            jax.ShapeDtypeStruct((T, GATE_LANES), F32),
            jax.ShapeDtypeStruct((B, 2 * SUBLANES, S), F32),
        ],
        compiler_params=_params(("parallel",)),
        name="gates",
    )(xb, wg, bias)


def _mlstm_kernel(q_ref, k_ref, v_ref, o_ref, gcol_ref, grow_ref, na_ref, out_ref,
                  c_ref, n_ref, m_ref):
    L = A_CHUNK
    dh = A_HEAD_DIM

    @pl.when(pl.program_id(1) == 0)
    def _():
        c_ref[...] = jnp.zeros_like(c_ref)
        n_ref[...] = jnp.zeros_like(n_ref)
        m_ref[...] = jnp.zeros_like(m_ref)

    ti = lax.broadcasted_iota(jnp.int32, (L, L), 0)
    si = lax.broadcasted_iota(jnp.int32, (L, L), 1)
    causal = si <= ti
    gcol = gcol_ref[...]
    grow = grow_ref[0]
    for h in range(A_HEADS):
        cs = slice(h * dh, (h + 1) * dh)
        q = q_ref[:, cs]
        k = k_ref[:, cs]
        v = v_ref[:, cs]
        i_col = gcol[:, h:h + 1]
        b_col = gcol[:, A_HEADS + h:A_HEADS + h + 1]
        i_row = grow[h:h + 1, :]
        b_row = grow[A_HEADS + h:A_HEADS + h + 1, :]
        g = b_col[L - 1:L, :]
        m = m_ref[h]
        dmat = jnp.where(causal, b_col - b_row + i_row, -jnp.inf)
        inter = b_col + m
        m_row = jnp.maximum(inter, jnp.max(dmat, axis=1, keepdims=True))
        w_intra = jnp.exp(dmat - m_row)
        w_inter = jnp.exp(inter - m_row)
        s = _dot_nt(q, k) * w_intra
        c_state = c_ref[h]
        num = w_inter * _dot(q, c_state.astype(BF16)) + _dot(s.astype(BF16), v)
        qn = jnp.sum(q.astype(F32) * n_ref[h], axis=1, keepdims=True)
        den = w_inter * qn + jnp.sum(s, axis=1, keepdims=True)
        hh = num / jnp.maximum(jnp.abs(den), jnp.exp(-m_row))
        a_col = g - b_col + i_col
        m_new = jnp.maximum(g + m, jnp.max(a_col, axis=0, keepdims=True))
        decay = jnp.exp(g + m - m_new)
        wk = jnp.exp(a_col - m_new) * k.astype(F32)
        c_ref[h] = decay * c_state + _dot_tn(wk.astype(BF16), v)
        n_ref[h] = decay * n_ref[h] + jnp.sum(wk, axis=0, keepdims=True)
        m_ref[h] = m_new
        mu = jnp.mean(hh, axis=1, keepdims=True)
        d = hh - mu
        var = jnp.mean(d * d, axis=1, keepdims=True)
        hn = d * lax.rsqrt(var + LN_EPS) * na_ref[:, cs] * o_ref[:, cs].astype(F32)
        out_ref[:, cs] = hn.astype(out_ref.dtype)


def _mlstm(proj, gcol, grow, norm_a, B, S):
    T = proj.shape[0]
    L = A_CHUNK
    nc = S // L
    blk = lambda cb: pl.BlockSpec((L, BW), lambda b, c, cb=cb: (b * nc + c, cb))
    return pl.pallas_call(
        _mlstm_kernel,
        grid=(B, nc),
        in_specs=[
            blk(0), blk(1), blk(2), blk(3),
            pl.BlockSpec((L, GATE_LANES), lambda b, c: (b * nc + c, 0)),
            pl.BlockSpec((1, 2 * SUBLANES, L), lambda b, c: (b, 0, c)),
            pl.BlockSpec((1, BW), lambda b, c: (0, 0)),
        ],
        out_specs=pl.BlockSpec((L, BW), lambda b, c: (b * nc + c, 0)),
        out_shape=jax.ShapeDtypeStruct((T, BW), BF16),
        scratch_shapes=[
            pltpu.VMEM((A_HEADS, A_HEAD_DIM, A_HEAD_DIM), F32),
            pltpu.VMEM((A_HEADS, 1, A_HEAD_DIM), F32),
            pltpu.VMEM((A_HEADS, 1, 1), F32),
        ],
        compiler_params=_params(("parallel", "arbitrary")),
        name="mlstm",
    )(proj, proj, proj, proj, gcol, grow, norm_a.reshape(1, BW))


def _pool_kernel(u_ref, wp_ref, ps_ref, out_ref, *, S):
    t = lax.broadcasted_iota(jnp.int32, (S, 1), 0)
    gd = B_GROUP_DIM
    for gi, w in enumerate(B_WINDOWS):
        cs = slice(gi * gd, (gi + 1) * gd)
        u = u_ref[:, cs].astype(F32)
        s = u
        k = 1
        while k < w:
            s = s + jnp.where(t >= k, pltpu.roll(s, k, axis=0), 0.0)
            k *= 2
        cnt = jnp.minimum(t + 1, w).astype(F32)
        pooled = s / cnt - u
        y = _dot(pooled.astype(BF16), wp_ref[gi]) * ps_ref[:, cs]
        out_ref[:, cs] = y.astype(out_ref.dtype)


def _pool(proj, w_pool, pool_scale, B, S):
    T = proj.shape[0]
    G = len(B_WINDOWS)
    return pl.pallas_call(
        functools.partial(_pool_kernel, S=S),
        grid=(B,),
        in_specs=[
            pl.BlockSpec((S, BW), lambda b: (b, 4)),
            pl.BlockSpec((G, B_GROUP_DIM, B_GROUP_DIM), lambda b: (0, 0, 0)),
            pl.BlockSpec((1, BW), lambda b: (0, 0)),
        ],
        out_specs=pl.BlockSpec((S, BW), lambda b: (b, 0)),
        out_shape=jax.ShapeDtypeStruct((T, BW), BF16),
        compiler_params=_params(("parallel",)),
        name="pool",
    )(proj, w_pool, pool_scale.reshape(1, BW))


def _fox_kernel(q_ref, k_ref, v_ref, gcol_ref, grow_ref, out_ref, m_ref, l_ref, acc_ref, *, tq):
    qi = pl.program_id(1)
    dh = C_HEAD_DIM
    scale = dh ** -0.5
    n_a = 2 * A_HEADS
    m_ref[...] = jnp.full(m_ref.shape, -jnp.inf, F32)
    l_ref[...] = jnp.zeros(l_ref.shape, F32)
    acc_ref[...] = jnp.zeros(acc_ref.shape, F32)

    def block(kj, masked):
        k0 = pl.multiple_of(kj * tq, tq)
        for h in range(C_HEADS):
            cs = slice(h * dh, (h + 1) * dh)
            k = k_ref[pl.ds(k0, tq), cs]
            v = v_ref[pl.ds(k0, tq), cs]
            c_col = gcol_ref[:, n_a + h:n_a + h + 1]
            c_row = grow_ref[0, n_a + h:n_a + h + 1, pl.ds(k0, tq)]
            lg = _dot_nt(q_ref[:, cs], k) * scale + (c_col - c_row)
            if masked:
                ri = lax.broadcasted_iota(jnp.int32, (tq, tq), 0)
                ci = lax.broadcasted_iota(jnp.int32, (tq, tq), 1)
                lg = jnp.where(ci <= ri, lg, -jnp.inf)
            m = m_ref[h]
            m_new = jnp.maximum(m, jnp.max(lg, axis=1, keepdims=True))
            p = jnp.exp(lg - m_new)
            alpha = jnp.exp(m - m_new)
            l_ref[h] = alpha * l_ref[h] + jnp.sum(p, axis=1, keepdims=True)
            acc_ref[h] = alpha * acc_ref[h] + _dot(p.astype(BF16), v)
            m_ref[h] = m_new

    def body(kj, carry):
        block(kj, False)
        return carry

    lax.fori_loop(0, qi, body, 0)
    block(qi, True)
    for h in range(C_HEADS):
        cs = slice(h * dh, (h + 1) * dh)
        out_ref[:, cs] = (acc_ref[h] / l_ref[h]).astype(out_ref.dtype)


def _fox(proj, gcol, grow, B, S):
    T = proj.shape[0]
    tq = min(FOX_TQ, S)
    nq = S // tq
    return pl.pallas_call(
        functools.partial(_fox_kernel, tq=tq),
        grid=(B, nq),
        in_specs=[
            pl.BlockSpec((tq, BW), lambda b, i: (b * nq + i, 5)),
            pl.BlockSpec((S, BW), lambda b, i: (b, 6)),
            pl.BlockSpec((S, BW), lambda b, i: (b, 7)),
            pl.BlockSpec((tq, GATE_LANES), lambda b, i: (b * nq + i, 0)),
            pl.BlockSpec((1, 2 * SUBLANES, S), lambda b, i: (b, 0, 0)),
        ],
        out_specs=pl.BlockSpec((tq, BW), lambda b, i: (b * nq + i, 0)),
        out_shape=jax.ShapeDtypeStruct((T, BW), BF16),
        scratch_shapes=[
            pltpu.VMEM((C_HEADS, tq, 1), F32),
            pltpu.VMEM((C_HEADS, tq, 1), F32),
            pltpu.VMEM((C_HEADS, tq, C_HEAD_DIM), F32),
        ],
        compiler_params=_params(("parallel", "arbitrary")),
        name="fox",
    )(proj, proj, proj, gcol, grow)


def _merge_kernel(ha_ref, hb_ref, hc_ref, g0_ref, g1_ref, g2_ref, wb_ref, wo_ref, x_ref,
                  lg_ref, lb_ref, xo_ref):
    merged = g0_ref[...].astype(F32) * _dot(ha_ref[...], wb_ref[0])
    merged = merged + g1_ref[...].astype(F32) * _dot(hb_ref[...], wb_ref[1])
    merged = merged + g2_ref[...].astype(F32) * _dot(hc_ref[...], wb_ref[2])
    h = _dot(merged.astype(BF16), wo_ref[...])
    xo_ref[...] = _ln_rows(ALPHA * x_ref[...] + h, lg_ref[...], lb_ref[...])


def _merge(ha, hb, hc, proj, w_branch, w_out, x, ln_g, ln_b):
    T, D = x.shape
    tm = MERGE_TM
    hblk = pl.BlockSpec((tm, BW), lambda i: (i, 0))
    gblk = lambda n: pl.BlockSpec((tm, D), lambda i, n=n: (i, 4 + n))
    const = lambda shape: pl.BlockSpec(shape, lambda i: (0,) * len(shape), pipeline_mode=pl.Buffered(1))
    row = pl.BlockSpec((tm, D), lambda i: (i, 0))
    return pl.pallas_call(
        _merge_kernel,
        grid=(T // tm,),
        in_specs=[hblk, hblk, hblk, gblk(0), gblk(1), gblk(2),
                  const((N_BRANCH, BW, D)), const((D, D)), row, const((1, D)), const((1, D))],
        out_specs=row,
        out_shape=jax.ShapeDtypeStruct((T, D), F32),
        compiler_params=_params(("parallel",)),
        name="merge",
    )(ha, hb, hc, proj, proj, proj, w_branch, w_out, x, ln_g.reshape(1, D), ln_b.reshape(1, D))


def _router_kernel(x_ref, wr_ref, br_ref, oi_ref, ow_ref):
    tm = x_ref.shape[0]
    ng, ne = N_GROUPS, EXPERTS_PER_GROUP
    lt = lax.dot_general(wr_ref[...], x_ref[...], (((1,), (1,)), ((), ())),
                         precision=lax.Precision.HIGHEST, preferred_element_type=F32)
    logit = [lt[ng * j:ng * (j + 1), :] for j in range(ne)]
    mx = functools.reduce(jnp.maximum, logit)
    mx = jnp.max(mx, axis=0, keepdims=True)
    ex = [jnp.exp(a - mx) for a in logit]
    den = jnp.sum(functools.reduce(jnp.add, ex), axis=0, keepdims=True)
    prob = [e / den for e in ex]
    sel = [prob[j] + br_ref[ng * j:ng * (j + 1), :] for j in range(ne)]

    def first_argmax(vals):
        best = functools.reduce(jnp.maximum, vals)
        idx = jnp.full(best.shape, float(ne - 1), F32)
        for j in range(ne - 2, -1, -1):
            idx = jnp.where(vals[j] == best, float(j), idx)
        return best, idx

    def pick(vals, idx):
        out = vals[ne - 1]
        for j in range(ne - 2, -1, -1):
            out = jnp.where(idx == float(j), vals[j], out)
        return out

    m1, i1 = first_argmax(sel)
    rest = [jnp.where(i1 == float(j), -jnp.inf, sel[j]) for j in range(ne)]
    m2, i2 = first_argmax(rest)
    score = m1 + m2
    gi = lax.broadcasted_iota(jnp.int32, (ng, tm), 0).astype(F32)
    gidx = jnp.min(jnp.where(score == jnp.max(score, axis=0, keepdims=True), gi, float(ng)),
                   axis=0, keepdims=True)
    chosen = gi == gidx
    take = lambda a: jnp.sum(jnp.where(chosen, a, 0.0), axis=0, keepdims=True)
    e1, e2 = take(i1), take(i2)
    p1, p2 = take(pick(prob, i1)), take(pick(prob, i2))
    psum = p1 + p2
    p1, p2 = p1 / psum, p2 / psum
    lo, hi = jnp.minimum(e1, e2), jnp.maximum(e1, e2)
    last = float(N_PAIRS - 1)
    pair = jnp.where(lo == 0.0, hi - 1.0, jnp.where(lo == 1.0, 6.0 - hi, last))
    w_lo = jnp.where(e1 < e2, p1, p2)
    w_hi = jnp.where(e1 < e2, p2, p1)
    w_a = jnp.where(pair == last, w_hi, w_lo)
    w_b = jnp.where(pair == last, w_lo, w_hi)
    zero = jnp.zeros((1, tm), F32)
    rows_i = [gidx * N_PAIRS + pair, gidx * ne + lo, gidx * ne + hi] + [zero] * (SUBLANES - 3)
    rows_w = [w_a, w_b] + [zero] * (SUBLANES - 2)
    oi_ref[...] = jnp.concatenate(rows_i, axis=0).astype(jnp.int32)
    ow_ref[...] = jnp.concatenate(rows_w, axis=0)


def _router(x, wr_t, br_t):
    T, D = x.shape
    tm = min(ROUTER_TM, T)
    return pl.pallas_call(
        _router_kernel,
        grid=(T // tm,),
        in_specs=[
            pl.BlockSpec((tm, D), lambda i: (i, 0)),
            pl.BlockSpec((N_EXPERTS, D), lambda i: (0, 0)),
            pl.BlockSpec((N_EXPERTS, 1), lambda i: (0, 0)),
        ],
        out_specs=[pl.BlockSpec((SUBLANES, tm), lambda i: (0, i))] * 2,
        out_shape=[jax.ShapeDtypeStruct((SUBLANES, T), jnp.int32),
                   jax.ShapeDtypeStruct((SUBLANES, T), F32)],
        compiler_params=_params(("parallel",)),
        name="router",
    )(x, wr_t, br_t)


def _moe_kernel(ea_ref, eb_ref, nrows_ref, src_ref, srcn_ref, wts_ref, x_hbm, w13a_ref, w13b_ref,
                w2a_ref, w2b_ref, lg_ref, lb_ref, out_hbm, xbuf, obuf, sem_in, sem_out, *, tm, nt):
    i = pl.program_id(0)
    slot = i % 2
    n_cur = nrows_ref[i]
    n_next = nrows_ref[jnp.minimum(i + 1, nt - 1)]
    n_prev = nrows_ref[jnp.maximum(i - 1, 0)]
    has_next = (i + 1 < nt) & (n_next > 0)
    has_prev = (i > 0) & (n_prev > 0)

    def gather(idx_ref, s):
        def start(r, c):
            tok = jnp.maximum(idx_ref[0, 0, r], 0)
            pltpu.make_async_copy(x_hbm.at[pl.ds(tok, 1)], xbuf.at[s, pl.ds(r, 1)],
                                  sem_in.at[s]).start()
            return c

        lax.fori_loop(0, tm, start, 0, unroll=MOE_DMA_UNROLL)

    def wait_gather(s):
        pltpu.make_async_copy(x_hbm.at[pl.ds(0, tm)], xbuf.at[s], sem_in.at[s]).wait()

    def wait_scatter(n):
        p = tm
        while p >= 1:
            @pl.when((n & p) != 0)
            def _(p=p):
                pltpu.make_async_copy(obuf.at[pl.ds(0, p)], out_hbm.at[pl.ds(0, p)], sem_out).wait()

            p //= 2

    @pl.when((i == 0) & (n_cur > 0))
    def _():
        gather(src_ref, 0)

    @pl.when(n_cur > 0)
    def _():
        wait_gather(slot)

    @pl.when(has_next)
    def _():
        gather(srcn_ref, 1 - slot)

    @pl.when(n_cur > 0)
    def _():
        x = xbuf[slot]
        xb = x.astype(BF16)

        def expert(w13_ref, w2_ref):
            h = _dot(xb, w13_ref[0])
            a = jax.nn.silu(h[:, :D_FF]) * h[:, D_FF:]
            return _dot(a.astype(BF16), w2_ref[0])

        y = wts_ref[:, 0:1] * expert(w13a_ref, w2a_ref)
        y = y + wts_ref[:, 1:2] * expert(w13b_ref, w2b_ref)
        z = _ln_rows(ALPHA * x + y, lg_ref[...], lb_ref[...])

        @pl.when(has_prev)
        def _():
            wait_scatter(n_prev)

        obuf[...] = z

        def start_out(r, c):
            tok = src_ref[0, 0, r]
            pltpu.make_async_copy(obuf.at[pl.ds(r, 1)], out_hbm.at[pl.ds(tok, 1)], sem_out).start()
            return c

        @pl.when(n_cur == tm)
        def _():
            lax.fori_loop(0, tm, start_out, 0, unroll=MOE_DMA_UNROLL)

        @pl.when(n_cur < tm)
        def _():
            lax.fori_loop(0, n_cur, start_out, 0)

    @pl.when((n_cur == 0) & has_prev)
    def _():
        wait_scatter(n_prev)

    @pl.when((i == nt - 1) & (n_cur > 0))
    def _():
        wait_scatter(n_cur)


def _moe(x, oi, ow, w13, w2, ln_g, ln_b):
    T, D = x.shape
    tm = MOE_TM
    npad = T + N_CLASSES * tm
    nt = npad // tm
    cls = oi[0]
    counts = jnp.zeros((N_CLASSES,), jnp.int32).at[cls].add(1)
    ptiles = (counts + tm - 1) // tm
    tile_end = jnp.cumsum(ptiles)
    tile_off = tile_end - ptiles
    row_off = jnp.cumsum(counts) - counts
    order = jnp.argsort(cls, stable=True).astype(jnp.int32)
    cls_sorted = cls[order]
    pos = tile_off[cls_sorted] * tm + (jnp.arange(T, dtype=jnp.int32) - row_off[cls_sorted])
    src = jnp.full((npad,), -1, jnp.int32).at[pos].set(order)
    tile_id = jnp.arange(nt, dtype=jnp.int32)
    n_used = tile_end[-1]
    tile_cls = jnp.searchsorted(tile_end, jnp.minimum(tile_id, n_used - 1), side="right").astype(jnp.int32)
    tile_rows = jnp.clip(counts[tile_cls] - (tile_id - tile_off[tile_cls]) * tm, 0, tm)
    tile_rows = jnp.where(tile_id < n_used, tile_rows, 0).astype(jnp.int32)
    pair = tile_cls % N_PAIRS
    tile_e1 = (tile_cls // N_PAIRS) * EXPERTS_PER_GROUP + jnp.array(PAIR_SLOT_A, jnp.int32)[pair]
    tile_e2 = (tile_cls // N_PAIRS) * EXPERTS_PER_GROUP + jnp.array(PAIR_SLOT_B, jnp.int32)[pair]
    wts = jnp.take(ow[0:2].T, jnp.maximum(src, 0), axis=0)
    src3 = src.reshape(nt, 1, tm)

    F2 = 2 * D_FF
    grid_spec = pltpu.PrefetchScalarGridSpec(
        num_scalar_prefetch=3,
        grid=(nt,),
        in_specs=[
            pl.BlockSpec((1, 1, tm), lambda i, e1, e2, va: (i, 0, 0), memory_space=pltpu.SMEM),
            pl.BlockSpec((1, 1, tm), lambda i, e1, e2, va: (jnp.minimum(i + 1, nt - 1), 0, 0),
                         memory_space=pltpu.SMEM),
            pl.BlockSpec((tm, 2), lambda i, e1, e2, va: (i, 0)),
            pl.BlockSpec(memory_space=pl.ANY),
            pl.BlockSpec((1, D, F2), lambda i, e1, e2, va: (e1[i], 0, 0)),
            pl.BlockSpec((1, D, F2), lambda i, e1, e2, va: (e2[i], 0, 0)),
            pl.BlockSpec((1, D_FF, D), lambda i, e1, e2, va: (e1[i], 0, 0)),
            pl.BlockSpec((1, D_FF, D), lambda i, e1, e2, va: (e2[i], 0, 0)),
            pl.BlockSpec((1, D), lambda i, e1, e2, va: (0, 0)),
            pl.BlockSpec((1, D), lambda i, e1, e2, va: (0, 0)),
        ],
        out_specs=pl.BlockSpec(memory_space=pl.ANY),
        scratch_shapes=[
            pltpu.VMEM((2, tm, D), F32),
            pltpu.VMEM((tm, D), F32),
            pltpu.SemaphoreType.DMA((2,)),
            pltpu.SemaphoreType.DMA(()),
        ],
    )
    return pl.pallas_call(
        functools.partial(_moe_kernel, tm=tm, nt=nt),
        grid_spec=grid_spec,
        out_shape=jax.ShapeDtypeStruct((T, D), F32),
        compiler_params=_params(("arbitrary",)),
        name="moe",
    )(tile_e1, tile_e2, tile_rows, src3, src3, wts, x, w13, w13, w2, w2,
      ln_g.reshape(1, D), ln_b.reshape(1, D))


def _pack_layer_weights(w_in, conv_a, b_if_a, b_f_c):
    a_end = OFF_A + 4 * BW
    w_main = jnp.concatenate(
        [w_in[:, OFF_A:a_end], w_in[:, OFF_B:OFF_B + BW], w_in[:, OFF_C:OFF_C + 3 * BW],
         w_in[:, OFF_G:OFF_G + N_G]], axis=1).astype(BF16)
    n_gate = 2 * A_HEADS + C_HEADS
    wg = jnp.concatenate([w_in[:, a_end:a_end + 2 * A_HEADS], w_in[:, OFF_C + 3 * BW:OFF_C + N_C]], axis=1)
    wg = jnp.pad(wg, ((0, 0), (0, GATE_LANES - n_gate))).astype(BF16)
    bias = jnp.pad(jnp.concatenate([b_if_a, b_f_c]), (0, GATE_LANES - n_gate)).reshape(1, GATE_LANES)
    k_scale = jnp.concatenate([jnp.ones((BW,), F32), jnp.full((BW,), A_HEAD_DIM ** -0.5, F32)])
    cw = jnp.concatenate([conv_a, k_scale[None, :], jnp.zeros((SUBLANES - A_CONV - 1, 2 * BW), F32)], axis=0)
    cw = jnp.pad(cw, ((0, 0), (0, N_MAIN - 2 * BW)))
    return w_main, wg, bias, cw


def _forward(x, ln0_g, ln0_b, w_in, conv_a, b_if_a, norm_a, w_pool, pool_scale, b_f_c,
             w_branch, w_out, ln1_g, ln1_b, w_router, b_router, w13, w2, ln2_g, ln2_b):
    B, S, D = x.shape
    T = B * S
    xf, xb = _ln0(x.reshape(T, D), ln0_g, ln0_b)
    perm = (jnp.arange(N_EXPERTS) % N_GROUPS) * EXPERTS_PER_GROUP + jnp.arange(N_EXPERTS) // N_GROUPS
    wr_t = w_router.T[perm]
    br_t = b_router[perm].reshape(N_EXPERTS, 1)
    for l in range(w_in.shape[0]):
        w_main, wg, bias, cw = _pack_layer_weights(w_in[l], conv_a[l], b_if_a[l], b_f_c[l])
        proj = _inproj(xb, w_main, cw, S)
        gcol, grow = _gates(xb, wg, bias, B, S)
        ha = _mlstm(proj, gcol, grow, norm_a[l], B, S)
        hb = _pool(proj, w_pool[l].astype(BF16), pool_scale[l], B, S)
        hc = _fox(proj, gcol, grow, B, S)
        x1 = _merge(ha, hb, hc, proj, w_branch[l].astype(BF16), w_out[l].astype(BF16), xf,
                    ln1_g[l], ln1_b[l])
        oi, ow = _router(x1, wr_t, br_t)
        xf = _moe(x1, oi, ow, w13[l].astype(BF16), w2[l].astype(BF16), ln2_g[l], ln2_b[l])
        xb = xf.astype(BF16)
    return xf.reshape(B, S, D)


def kernel(x, ln0_g, ln0_b, w_in, conv_a, b_if_a, norm_a, w_pool, pool_scale, b_f_c, w_branch,
           w_out, ln1_g, ln1_b, w_router, b_router, w13, w2, ln2_g, ln2_b):
    return _forward(x, ln0_g, ln0_b, w_in, conv_a, b_if_a, norm_a, w_pool, pool_scale, b_f_c,
                    w_branch, w_out, ln1_g, ln1_b, w_router, b_router, w13, w2, ln2_g, ln2_b)
```

```python
import functools

import jax
import jax.numpy as jnp
from jax import lax
from jax.experimental import pallas as pl
from jax.experimental.pallas import tpu as pltpu

F32 = jnp.float32
BF16 = jnp.bfloat16

D_MODEL = 2048
DEPTH = 2
BW = D_MODEL // 2
A_HEADS = 4
A_HEAD_DIM = BW // A_HEADS
A_CONV = 4
A_CHUNK = 128
B_WINDOWS = (2, 4, 8, 16)
B_GROUP_DIM = BW // len(B_WINDOWS)
C_HEADS = 8
C_HEAD_DIM = BW // C_HEADS
N_BRANCH = 3
OFF_A = 0
N_A = 4 * BW + 2 * A_HEADS
OFF_B = OFF_A + N_A
OFF_C = OFF_B + BW
N_C = 3 * BW + C_HEADS
OFF_G = OFF_C + N_C
N_G = N_BRANCH * D_MODEL
N_EXPERTS = 32
N_GROUPS = 8
EXPERTS_PER_GROUP = N_EXPERTS // N_GROUPS
D_FF = D_MODEL * 3 // 8
ALPHA = (2 * DEPTH) ** 0.25
LN_EPS = 1e-5

N_MAIN = 4 * BW + N_G + BW + 2 * BW
PB_GATES = 4
PB_POOL = 10
PB_CQ = 11
PB_CK = 12
GATE_LANES = 128
N_PAIRS = 6
N_CLASSES = N_GROUPS * N_PAIRS
PAIR_SLOT_A = (0, 0, 0, 1, 1, 3)
PAIR_SLOT_B = (1, 2, 3, 3, 2, 2)

LANES = 128
SUBLANES = 8
VMEM_LIMIT_BYTES = 56 * 1024 * 1024

IN_TN = 1024
IN_RM = 256
EPI_CB = 256
MERGE_TM = 256
ROUTER_TM = 512
MOE_TM = 256
MOE_DMA_UNROLL = 8
FOX_TQ = 256
LN0_TM = 512


def _params(sem, vmem=VMEM_LIMIT_BYTES):
    return pltpu.CompilerParams(dimension_semantics=sem, vmem_limit_bytes=vmem)


def _ln_rows(y, g, b):
    mu = jnp.mean(y, axis=-1, keepdims=True)
    d = y - mu
    var = jnp.mean(d * d, axis=-1, keepdims=True)
    return d * lax.rsqrt(var + LN_EPS) * g + b


def _dot(a, b):
    return jnp.dot(a, b, preferred_element_type=F32)


def _dot_nt(a, b):
    return lax.dot_general(a, b, (((1,), (1,)), ((), ())), preferred_element_type=F32)


def _dot_tn(a, b):
    return lax.dot_general(a, b, (((0,), (0,)), ((), ())), preferred_element_type=F32)


def _ln0_kernel(x_ref, g_ref, b_ref, of_ref, ob_ref):
    y = _ln_rows(x_ref[...], g_ref[...], b_ref[...])
    of_ref[...] = y
    ob_ref[...] = y.astype(BF16)


def _ln0(x2d, g, b):
    T, D = x2d.shape
    tm = LN0_TM
    row = pl.BlockSpec((tm, D), lambda i: (i, 0))
    vec = pl.BlockSpec((1, D), lambda i: (0, 0))
    return pl.pallas_call(
        _ln0_kernel,
        grid=(T // tm,),
        in_specs=[row, vec, vec],
        out_specs=[row, row],
        out_shape=[jax.ShapeDtypeStruct((T, D), F32), jax.ShapeDtypeStruct((T, D), BF16)],
        compiler_params=_params(("parallel",)),
        name="ln0",
    )(x2d, g.reshape(1, D), b.reshape(1, D))


def _inproj_kernel(x_ref, w_ref, cw_ref, o_ref, acc_ref, *, segments, tm, tn):
    j = pl.program_id(1)
    rm = min(IN_RM, tm)

    def chunk(mi, kind):
        r0 = pl.multiple_of(mi * rm, rm)
        acc = _dot(x_ref[pl.ds(r0, rm), :], w_ref[...])
        if kind == "none":
            o_ref[pl.ds(r0, rm), :] = acc.astype(o_ref.dtype)
        elif kind == "sigmoid":
            o_ref[pl.ds(r0, rm), :] = jax.nn.sigmoid(acc).astype(o_ref.dtype)
        else:
            @pl.when(mi == 0)
            def _():
                acc_ref[0:SUBLANES, :] = jnp.zeros((SUBLANES, tn), F32)

            acc_ref[SUBLANES:, :] = acc
            for cb in range(tn // EPI_CB):
                cs = slice(cb * EPI_CB, (cb + 1) * EPI_CB)
                y = cw_ref[A_CONV - 1:A_CONV, cs] * acc_ref[SUBLANES:SUBLANES + rm, cs]
                for d in range(1, A_CONV):
                    tap = cw_ref[A_CONV - 1 - d:A_CONV - d, cs]
                    y = y + tap * acc_ref[SUBLANES - d:SUBLANES - d + rm, cs]
                y = y * jax.nn.sigmoid(y) * cw_ref[A_CONV:A_CONV + 1, cs]
                o_ref[pl.ds(r0, rm), cs] = y.astype(o_ref.dtype)
            acc_ref[0:SUBLANES, :] = acc_ref[rm:rm + SUBLANES, :]

    for lo, hi, kind in segments:

        @pl.when((j >= lo) & (j < hi))
        def _():
            def body(mi, carry):
                chunk(mi, kind)
                return carry

            lax.fori_loop(0, tm // rm, body, 0)


def _inproj(xb, w_main, cw, S):
    T, K = xb.shape
    N = w_main.shape[1]
    tm, tn = S, IN_TN
    t = lambda c: c // tn
    segments = (
        (t(0), t(2 * BW), "conv"),
        (t(2 * BW), t(3 * BW), "none"),
        (t(3 * BW), t(PB_POOL * BW), "sigmoid"),
        (t(PB_POOL * BW), t(N), "none"),
    )
    return pl.pallas_call(
        functools.partial(_inproj_kernel, segments=segments, tm=tm, tn=tn),
        grid=(T // tm, N // tn),
        in_specs=[
            pl.BlockSpec((tm, K), lambda i, j: (i, 0)),
            pl.BlockSpec((K, tn), lambda i, j: (0, j)),
            pl.BlockSpec((SUBLANES, tn), lambda i, j: (0, j)),
        ],
        out_specs=pl.BlockSpec((tm, tn), lambda i, j: (i, j)),
        out_shape=jax.ShapeDtypeStruct((T, N), BF16),
        scratch_shapes=[pltpu.VMEM((min(IN_RM, tm) + SUBLANES, tn), F32)],
        compiler_params=_params(("parallel", "arbitrary")),
        name="inproj",
    )(xb, w_main, cw)


def _gates_kernel(x_ref, wg_ref, bias_ref, gcol_ref, grow_ref, *, S):
    g = _dot(x_ref[...], wg_ref[...]) + bias_ref[...]
    col = lax.broadcasted_iota(jnp.int32, (1, GATE_LANES), 1)
    n_a = 2 * A_HEADS
    logf = jnp.where((col >= A_HEADS) & (col < n_a + C_HEADS), jax.nn.log_sigmoid(g), 0.0)
    nch = S // A_CHUNK
    cat = jnp.concatenate([logf[c * A_CHUNK:(c + 1) * A_CHUNK, :] for c in range(nch)], axis=1)
    ri = lax.broadcasted_iota(jnp.int32, (A_CHUNK, A_CHUNK), 0)
    ci = lax.broadcasted_iota(jnp.int32, (A_CHUNK, A_CHUNK), 1)
    tri = jnp.where(ri >= ci, 1.0, 0.0).astype(BF16)
    hi = cat.astype(BF16)
    r1 = cat - hi.astype(F32)
    mid = r1.astype(BF16)
    low = (r1 - mid.astype(F32)).astype(BF16)
    within = _dot(tri, hi) + _dot(tri, mid) + _dot(tri, low)
    carry = jnp.zeros((1, GATE_LANES), F32)
    for c in range(nch):
        rows = slice(c * A_CHUNK, (c + 1) * A_CHUNK)
        wc = within[:, c * GATE_LANES:(c + 1) * GATE_LANES]
        glob = wc + carry
        carry = carry + wc[A_CHUNK - 1:A_CHUNK, :]
        gcol_ref[rows, :] = jnp.where(col < A_HEADS, g[rows, :], jnp.where(col < n_a, wc, glob))
    grow_ref[0] = gcol_ref[...].T[0:2 * SUBLANES, :]


def _gates(xb, wg, bias, B, S):
    T, K = xb.shape
    return pl.pallas_call(
        functools.partial(_gates_kernel, S=S),
        grid=(B,),
        in_specs=[
            pl.BlockSpec((S, K), lambda b: (b, 0)),
            pl.BlockSpec((K, GATE_LANES), lambda b: (0, 0)),
            pl.BlockSpec((1, GATE_LANES), lambda b: (0, 0)),
        ],
        out_specs=[
            pl.BlockSpec((S, GATE_LANES), lambda b: (b, 0)),
            pl.BlockSpec((1, 2 * SUBLANES, S), lambda b: (b, 0, 0)),
        ],
        out_shape=[
            jax.ShapeDtypeStruct((T, GATE_LANES), F32),
            jax.ShapeDtypeStruct((B, 2 * SUBLANES, S), F32),
        ],
        compiler_params=_params(("parallel",)),
        name="gates",
    )(xb, wg, bias)


def _mlstm_kernel(q_ref, k_ref, v_ref, o_ref, gcol_ref, grow_ref, na_ref, out_ref,
                  c_ref, n_ref, m_ref):
    L = A_CHUNK
    dh = A_HEAD_DIM

    @pl.when(pl.program_id(1) == 0)
    def _():
        c_ref[...] = jnp.zeros_like(c_ref)
        n_ref[...] = jnp.zeros_like(n_ref)
        m_ref[...] = jnp.zeros_like(m_ref)

    ti = lax.broadcasted_iota(jnp.int32, (L, L), 0)
    si = lax.broadcasted_iota(jnp.int32, (L, L), 1)
    causal = si <= ti
    gcol = gcol_ref[...]
    grow = grow_ref[0]
    for h in range(A_HEADS):
        cs = slice(h * dh, (h + 1) * dh)
        q = q_ref[:, cs]
        k = k_ref[:, cs]
        v = v_ref[:, cs]
        i_col = gcol[:, h:h + 1]
        b_col = gcol[:, A_HEADS + h:A_HEADS + h + 1]
        i_row = grow[h:h + 1, :]
        b_row = grow[A_HEADS + h:A_HEADS + h + 1, :]
        g = b_col[L - 1:L, :]
        m = m_ref[h]
        dmat = jnp.where(causal, b_col - b_row + i_row, -jnp.inf)
        inter = b_col + m
        m_row = jnp.maximum(inter, jnp.max(dmat, axis=1, keepdims=True))
        w_intra = jnp.exp(dmat - m_row)
        w_inter = jnp.exp(inter - m_row)
        s = _dot_nt(q, k) * w_intra
        c_state = c_ref[h]
        num = w_inter * _dot(q, c_state.astype(BF16)) + _dot(s.astype(BF16), v)
        qn = jnp.sum(q.astype(F32) * n_ref[h], axis=1, keepdims=True)
        den = w_inter * qn + jnp.sum(s, axis=1, keepdims=True)
        hh = num / jnp.maximum(jnp.abs(den), jnp.exp(-m_row))
        a_col = g - b_col + i_col
        m_new = jnp.maximum(g + m, jnp.max(a_col, axis=0, keepdims=True))
        decay = jnp.exp(g + m - m_new)
        wk = jnp.exp(a_col - m_new) * k.astype(F32)
        c_ref[h] = decay * c_state + _dot_tn(wk.astype(BF16), v)
        n_ref[h] = decay * n_ref[h] + jnp.sum(wk, axis=0, keepdims=True)
        m_ref[h] = m_new
        mu = jnp.mean(hh, axis=1, keepdims=True)
        d = hh - mu
        var = jnp.mean(d * d, axis=1, keepdims=True)
        hn = d * lax.rsqrt(var + LN_EPS) * na_ref[:, cs] * o_ref[:, cs].astype(F32)
        out_ref[:, cs] = hn.astype(out_ref.dtype)


def _mlstm(proj, gcol, grow, norm_a, B, S):
    T = proj.shape[0]
    L = A_CHUNK
    nc = S // L
    blk = lambda cb: pl.BlockSpec((L, BW), lambda b, c, cb=cb: (b * nc + c, cb))
    return pl.pallas_call(
        _mlstm_kernel,
        grid=(B, nc),
        in_specs=[
            blk(0), blk(1), blk(2), blk(3),
            pl.BlockSpec((L, GATE_LANES), lambda b, c: (b * nc + c, 0)),
            pl.BlockSpec((1, 2 * SUBLANES, L), lambda b, c: (b, 0, c)),
            pl.BlockSpec((1, BW), lambda b, c: (0, 0)),
        ],
        out_specs=pl.BlockSpec((L, BW), lambda b, c: (b * nc + c, 0)),
        out_shape=jax.ShapeDtypeStruct((T, BW), BF16),
        scratch_shapes=[
            pltpu.VMEM((A_HEADS, A_HEAD_DIM, A_HEAD_DIM), F32),
            pltpu.VMEM((A_HEADS, 1, A_HEAD_DIM), F32),
            pltpu.VMEM((A_HEADS, 1, 1), F32),
        ],
        compiler_params=_params(("parallel", "arbitrary")),
        name="mlstm",
    )(proj, proj, proj, proj, gcol, grow, norm_a.reshape(1, BW))


def _pool_kernel(u_ref, wp_ref, ps_ref, out_ref, *, S):
    t = lax.broadcasted_iota(jnp.int32, (S, 1), 0)
    gd = B_GROUP_DIM
    for gi, w in enumerate(B_WINDOWS):
        cs = slice(gi * gd, (gi + 1) * gd)
        u = u_ref[:, cs].astype(F32)
        s = u
        k = 1
        while k < w:
            s = s + jnp.where(t >= k, pltpu.roll(s, k, axis=0), 0.0)
            k *= 2
        cnt = jnp.minimum(t + 1, w).astype(F32)
        pooled = s / cnt - u
        y = _dot(pooled.astype(BF16), wp_ref[gi]) * ps_ref[:, cs]
        out_ref[:, cs] = y.astype(out_ref.dtype)


def _pool(proj, w_pool, pool_scale, B, S):
    T = proj.shape[0]
    G = len(B_WINDOWS)
    return pl.pallas_call(
        functools.partial(_pool_kernel, S=S),
        grid=(B,),
        in_specs=[
            pl.BlockSpec((S, BW), lambda b: (b, PB_POOL)),
            pl.BlockSpec((G, B_GROUP_DIM, B_GROUP_DIM), lambda b: (0, 0, 0)),
            pl.BlockSpec((1, BW), lambda b: (0, 0)),
        ],
        out_specs=pl.BlockSpec((S, BW), lambda b: (b, 0)),
        out_shape=jax.ShapeDtypeStruct((T, BW), BF16),
        compiler_params=_params(("parallel",)),
        name="pool",
    )(proj, w_pool, pool_scale.reshape(1, BW))


def _vt_kernel(x_ref, wt_ref, o_ref, *, S):
    rm = min(IN_RM, S)

    def body(mi, carry):
        r0 = pl.multiple_of(mi * rm, rm)
        o_ref[0, :, pl.ds(r0, rm)] = _dot_nt(wt_ref[...], x_ref[pl.ds(r0, rm), :]).astype(o_ref.dtype)
        return carry

    lax.fori_loop(0, S // rm, body, 0)


def _vt(xb, wv_t, B, S):
    K = xb.shape[1]
    return pl.pallas_call(
        functools.partial(_vt_kernel, S=S),
        grid=(B,),
        in_specs=[
            pl.BlockSpec((S, K), lambda b: (b, 0)),
            pl.BlockSpec((BW, K), lambda b: (0, 0)),
        ],
        out_specs=pl.BlockSpec((1, BW, S), lambda b: (b, 0, 0)),
        out_shape=jax.ShapeDtypeStruct((B, BW, S), BF16),
        compiler_params=_params(("parallel",)),
        name="vt",
    )(xb, wv_t)


def _fox_kernel(q_ref, k_ref, vt_ref, gcol_ref, grow_ref, out_ref, m_ref, l_ref, a_ref, acc_ref,
                s_ref, p_ref, *, tq):
    qi = pl.program_id(1)
    dh = C_HEAD_DIM
    scale = dh ** -0.5
    n_a = 2 * A_HEADS
    m_ref[...] = jnp.full(m_ref.shape, -jnp.inf, F32)
    l_ref[...] = jnp.zeros(l_ref.shape, F32)
    acc_ref[...] = jnp.zeros(acc_ref.shape, F32)

    def block(kj, masked):
        k0 = pl.multiple_of(kj * tq, tq)
        for h in range(C_HEADS):
            cs = slice(h * dh, (h + 1) * dh)
            s_ref[h] = _dot_nt(k_ref[pl.ds(k0, tq), cs], q_ref[:, cs])
        for h in range(C_HEADS):
            c_key = gcol_ref[pl.ds(k0, tq), n_a + h:n_a + h + 1]
            c_qry = grow_ref[0, n_a + h:n_a + h + 1, :]
            lg = s_ref[h] * scale + (c_qry - c_key)
            if masked:
                si = lax.broadcasted_iota(jnp.int32, (tq, tq), 0)
                ti = lax.broadcasted_iota(jnp.int32, (tq, tq), 1)
                lg = jnp.where(si <= ti, lg, -jnp.inf)
            s_ref[h] = lg
            m = m_ref[h]
            m_new = jnp.maximum(m, jnp.max(lg, axis=0, keepdims=True))
            a_ref[h] = jnp.exp(m - m_new)
            m_ref[h] = m_new
        for h in range(C_HEADS):
            p = jnp.exp(s_ref[h] - m_ref[h])
            l_ref[h] = a_ref[h] * l_ref[h] + jnp.sum(p, axis=0, keepdims=True)
            p_ref[h] = p.astype(BF16)
        for h in range(C_HEADS):
            cs = slice(h * dh, (h + 1) * dh)
            vt = vt_ref[0, cs, pl.ds(k0, tq)]
            acc_ref[h] = a_ref[h] * acc_ref[h] + _dot(vt, p_ref[h])

    def body(kj, carry):
        block(kj, False)
        return carry

    lax.fori_loop(0, qi, body, 0)
    block(qi, True)
    for h in range(C_HEADS):
        cs = slice(h * dh, (h + 1) * dh)
        out_ref[:, cs] = (acc_ref[h] / l_ref[h]).T.astype(out_ref.dtype)


def _fox(proj, vt, gcol, grow, B, S):
    T = proj.shape[0]
    tq = min(FOX_TQ, S)
    nq = S // tq
    return pl.pallas_call(
        functools.partial(_fox_kernel, tq=tq),
        grid=(B, nq),
        in_specs=[
            pl.BlockSpec((tq, BW), lambda b, i: (b * nq + i, PB_CQ)),
            pl.BlockSpec((S, BW), lambda b, i: (b, PB_CK)),
            pl.BlockSpec((1, BW, S), lambda b, i: (b, 0, 0)),
            pl.BlockSpec((S, GATE_LANES), lambda b, i: (b, 0)),
            pl.BlockSpec((1, 2 * SUBLANES, tq), lambda b, i: (b, 0, i)),
        ],
        out_specs=pl.BlockSpec((tq, BW), lambda b, i: (b * nq + i, 0)),
        out_shape=jax.ShapeDtypeStruct((T, BW), BF16),
        scratch_shapes=[
            pltpu.VMEM((C_HEADS, 1, tq), F32),
            pltpu.VMEM((C_HEADS, 1, tq), F32),
            pltpu.VMEM((C_HEADS, 1, tq), F32),
            pltpu.VMEM((C_HEADS, C_HEAD_DIM, tq), F32),
            pltpu.VMEM((C_HEADS, tq, tq), F32),
            pltpu.VMEM((C_HEADS, tq, tq), BF16),
        ],
        compiler_params=_params(("parallel", "arbitrary")),
        name="fox",
    )(proj, proj, vt, gcol, grow)


def _merge_kernel(ha_ref, hb_ref, hc_ref, g0_ref, g1_ref, g2_ref, wb_ref, wo_ref, x_ref,
                  lg_ref, lb_ref, xo_ref):
    merged = g0_ref[...].astype(F32) * _dot(ha_ref[...], wb_ref[0])
    merged = merged + g1_ref[...].astype(F32) * _dot(hb_ref[...], wb_ref[1])
    merged = merged + g2_ref[...].astype(F32) * _dot(hc_ref[...], wb_ref[2])
    h = _dot(merged.astype(BF16), wo_ref[...])
    xo_ref[...] = _ln_rows(ALPHA * x_ref[...] + h, lg_ref[...], lb_ref[...])


def _merge(ha, hb, hc, proj, w_branch, w_out, x, ln_g, ln_b):
    T, D = x.shape
    tm = MERGE_TM
    hblk = pl.BlockSpec((tm, BW), lambda i: (i, 0))
    gblk = lambda n: pl.BlockSpec((tm, D), lambda i, n=n: (i, PB_GATES * BW // D + n))
    const = lambda shape: pl.BlockSpec(shape, lambda i: (0,) * len(shape), pipeline_mode=pl.Buffered(1))
    row = pl.BlockSpec((tm, D), lambda i: (i, 0))
    return pl.pallas_call(
        _merge_kernel,
        grid=(T // tm,),
        in_specs=[hblk, hblk, hblk, gblk(0), gblk(1), gblk(2),
                  const((N_BRANCH, BW, D)), const((D, D)), row, const((1, D)), const((1, D))],
        out_specs=row,
        out_shape=jax.ShapeDtypeStruct((T, D), F32),
        compiler_params=_params(("parallel",)),
        name="merge",
    )(ha, hb, hc, proj, proj, proj, w_branch, w_out, x, ln_g.reshape(1, D), ln_b.reshape(1, D))


def _router_kernel(x_ref, wr_ref, br_ref, oi_ref, cnt_ref, xe_ref):
    tm = x_ref.shape[0]
    ng, ne = N_GROUPS, EXPERTS_PER_GROUP
    lt = lax.dot_general(wr_ref[...], x_ref[...], (((1,), (1,)), ((), ())),
                         precision=lax.Precision.HIGHEST, preferred_element_type=F32)
    logit = [lt[ng * j:ng * (j + 1), :] for j in range(ne)]
    mx = functools.reduce(jnp.maximum, logit)
    mx = jnp.max(mx, axis=0, keepdims=True)
    ex = [jnp.exp(a - mx) for a in logit]
    den = jnp.sum(functools.reduce(jnp.add, ex), axis=0, keepdims=True)
    prob = [e / den for e in ex]
    sel = [prob[j] + br_ref[ng * j:ng * (j + 1), :] for j in range(ne)]

    def first_argmax(vals):
        best = functools.reduce(jnp.maximum, vals)
        idx = jnp.full(best.shape, float(ne - 1), F32)
        for j in range(ne - 2, -1, -1):
            idx = jnp.where(vals[j] == best, float(j), idx)
        return best, idx

    def pick(vals, idx):
        out = vals[ne - 1]
        for j in range(ne - 2, -1, -1):
            out = jnp.where(idx == float(j), vals[j], out)
        return out

    m1, i1 = first_argmax(sel)
    rest = [jnp.where(i1 == float(j), -jnp.inf, sel[j]) for j in range(ne)]
    m2, i2 = first_argmax(rest)
    score = m1 + m2
    gi = lax.broadcasted_iota(jnp.int32, (ng, tm), 0).astype(F32)
    gidx = jnp.min(jnp.where(score == jnp.max(score, axis=0, keepdims=True), gi, float(ng)),
                   axis=0, keepdims=True)
    chosen = gi == gidx
    take = lambda a: jnp.sum(jnp.where(chosen, a, 0.0), axis=0, keepdims=True)
    e1, e2 = take(i1), take(i2)
    p1, p2 = take(pick(prob, i1)), take(pick(prob, i2))
    psum = p1 + p2
    p1, p2 = p1 / psum, p2 / psum
    lo, hi = jnp.minimum(e1, e2), jnp.maximum(e1, e2)
    last = float(N_PAIRS - 1)
    pair = jnp.where(lo == 0.0, hi - 1.0, jnp.where(lo == 1.0, 6.0 - hi, last))
    w_lo = jnp.where(e1 < e2, p1, p2)
    w_hi = jnp.where(e1 < e2, p2, p1)
    w_a = jnp.where(pair == last, w_hi, w_lo)
    w_b = jnp.where(pair == last, w_lo, w_hi)
    cls = gidx * N_PAIRS + pair
    @pl.when(pl.program_id(0) == 0)
    def _():
        cnt_ref[...] = jnp.zeros_like(cnt_ref)

    onehot = lax.broadcasted_iota(jnp.int32, (N_CLASSES, tm), 0).astype(F32) == cls
    jr = lax.broadcasted_iota(jnp.int32, (tm, tm), 0)
    tc = lax.broadcasted_iota(jnp.int32, (tm, tm), 1)
    upper = jnp.where(jr <= tc, 1.0, 0.0).astype(BF16)
    cum = _dot(jnp.where(onehot, 1.0, 0.0).astype(BF16), upper)
    before = cnt_ref[:, 0:1]
    rank = jnp.sum(jnp.where(onehot, cum - 1.0 + before, 0.0), axis=0, keepdims=True)
    cnt_ref[...] = cnt_ref[...] + cum[:, tm - 1:tm]
    zero = jnp.zeros((1, tm), F32)
    rows_i = [cls, gidx * ne + lo, gidx * ne + hi, rank] + [zero] * (SUBLANES - 4)
    oi_ref[...] = jnp.concatenate(rows_i, axis=0).astype(jnp.int32)
    d = x_ref.shape[1]
    xe_ref[:, 0:d] = x_ref[...]
    extra = jnp.concatenate([w_a, w_b, jnp.zeros((LANES - 2, tm), F32)], axis=0)
    xe_ref[:, d:] = extra.T


def _router(x, wr_t, br_t):
    T, D = x.shape
    tm = min(ROUTER_TM, T)
    return pl.pallas_call(
        _router_kernel,
        grid=(T // tm,),
        in_specs=[
            pl.BlockSpec((tm, D), lambda i: (i, 0)),
            pl.BlockSpec((N_EXPERTS, D), lambda i: (0, 0)),
            pl.BlockSpec((N_EXPERTS, 1), lambda i: (0, 0)),
        ],
        out_specs=[
            pl.BlockSpec((SUBLANES, tm), lambda i: (0, i)),
            pl.BlockSpec((N_CLASSES, LANES), lambda i: (0, 0)),
            pl.BlockSpec((tm, D + LANES), lambda i: (i, 0)),
        ],
        out_shape=[
            jax.ShapeDtypeStruct((SUBLANES, T), jnp.int32),
            jax.ShapeDtypeStruct((N_CLASSES, LANES), F32),
            jax.ShapeDtypeStruct((T, D + LANES), F32),
        ],
        compiler_params=_params(("arbitrary",)),
        name="router",
    )(x, wr_t, br_t)


def _moe_kernel(ea_ref, eb_ref, nrows_ref, src_ref, srcn_ref, x_hbm, w13a_ref, w13b_ref,
                w2a_ref, w2b_ref, lg_ref, lb_ref, out_hbm, xbuf, obuf, sem_in, sem_out, *, tm, nt):
    i = pl.program_id(0)
    slot = i % 2
    n_cur = nrows_ref[i]
    n_next = nrows_ref[jnp.minimum(i + 1, nt - 1)]
    n_prev = nrows_ref[jnp.maximum(i - 1, 0)]
    has_next = (i + 1 < nt) & (n_next > 0)
    has_prev = (i > 0) & (n_prev > 0)

    def gather(idx_ref, s):
        def start(r, c):
            tok = jnp.maximum(idx_ref[0, 0, r], 0)
            pltpu.make_async_copy(x_hbm.at[pl.ds(tok, 1)], xbuf.at[s, pl.ds(r, 1)],
                                  sem_in.at[s]).start()
            return c

        lax.fori_loop(0, tm, start, 0, unroll=MOE_DMA_UNROLL)

    def wait_gather(s):
        pltpu.make_async_copy(x_hbm.at[pl.ds(0, tm)], xbuf.at[s], sem_in.at[s]).wait()

    def wait_scatter(n):
        p = tm
        while p >= 1:
            @pl.when((n & p) != 0)
            def _(p=p):
                pltpu.make_async_copy(obuf.at[pl.ds(0, p)], out_hbm.at[pl.ds(0, p)], sem_out).wait()

            p //= 2

    @pl.when((i == 0) & (n_cur > 0))
    def _():
        gather(src_ref, 0)

    @pl.when(n_cur > 0)
    def _():
        wait_gather(slot)

    @pl.when(has_next)
    def _():
        gather(srcn_ref, 1 - slot)

    @pl.when(n_cur > 0)
    def _():
        d = out_hbm.shape[1]
        x = xbuf[slot, :, 0:d]
        w_a = xbuf[slot, :, d:d + 1]
        w_b = xbuf[slot, :, d + 1:d + 2]
        xb = x.astype(BF16)

        def expert(w13_ref, w2_ref):
            h = _dot(xb, w13_ref[0])
            a = jax.nn.silu(h[:, :D_FF]) * h[:, D_FF:]
            return _dot(a.astype(BF16), w2_ref[0])

        y = w_a * expert(w13a_ref, w2a_ref)
        y = y + w_b * expert(w13b_ref, w2b_ref)
        z = _ln_rows(ALPHA * x + y, lg_ref[...], lb_ref[...])

        @pl.when(has_prev)
        def _():
            wait_scatter(n_prev)

        obuf[...] = z

        def start_out(r, c):
            tok = src_ref[0, 0, r]
            pltpu.make_async_copy(obuf.at[pl.ds(r, 1)], out_hbm.at[pl.ds(tok, 1)], sem_out).start()
            return c

        @pl.when(n_cur == tm)
        def _():
            lax.fori_loop(0, tm, start_out, 0, unroll=MOE_DMA_UNROLL)

        @pl.when(n_cur < tm)
        def _():
            lax.fori_loop(0, n_cur, start_out, 0)

    @pl.when((n_cur == 0) & has_prev)
    def _():
        wait_scatter(n_prev)

    @pl.when((i == nt - 1) & (n_cur > 0))
    def _():
        wait_scatter(n_cur)


def _moe(xe, oi, cnt, w13, w2, ln_g, ln_b):
    T = xe.shape[0]
    D = D_MODEL
    tm = MOE_TM
    npad = T + N_CLASSES * tm
    nt = npad // tm
    cls, rank = oi[0], oi[3]
    counts = cnt[:, 0].astype(jnp.int32)
    ptiles = (counts + tm - 1) // tm
    tile_end = jnp.cumsum(ptiles)
    tile_off = tile_end - ptiles
    in_class = cls[:, None] == jnp.arange(N_CLASSES, dtype=jnp.int32)[None, :]
    pos = jnp.sum(jnp.where(in_class, tile_off[None, :], 0), axis=1) * tm + rank
    src = jnp.full((npad,), -1, jnp.int32).at[pos].set(jnp.arange(T, dtype=jnp.int32))
    tile_id = jnp.arange(nt, dtype=jnp.int32)
    n_used = tile_end[-1]
    tile_cls = jnp.searchsorted(tile_end, jnp.minimum(tile_id, n_used - 1), side="right").astype(jnp.int32)
    tile_rows = jnp.clip(counts[tile_cls] - (tile_id - tile_off[tile_cls]) * tm, 0, tm)
    tile_rows = jnp.where(tile_id < n_used, tile_rows, 0).astype(jnp.int32)
    pair = tile_cls % N_PAIRS
    tile_e1 = (tile_cls // N_PAIRS) * EXPERTS_PER_GROUP + jnp.array(PAIR_SLOT_A, jnp.int32)[pair]
    tile_e2 = (tile_cls // N_PAIRS) * EXPERTS_PER_GROUP + jnp.array(PAIR_SLOT_B, jnp.int32)[pair]
    src3 = src.reshape(nt, 1, tm)

    F2 = 2 * D_FF
    grid_spec = pltpu.PrefetchScalarGridSpec(
        num_scalar_prefetch=3,
        grid=(nt,),
        in_specs=[
            pl.BlockSpec((1, 1, tm), lambda i, e1, e2, va: (i, 0, 0), memory_space=pltpu.SMEM),
            pl.BlockSpec((1, 1, tm), lambda i, e1, e2, va: (jnp.minimum(i + 1, nt - 1), 0, 0),
                         memory_space=pltpu.SMEM),
            pl.BlockSpec(memory_space=pl.ANY),
            pl.BlockSpec((1, D, F2), lambda i, e1, e2, va: (e1[i], 0, 0)),
            pl.BlockSpec((1, D, F2), lambda i, e1, e2, va: (e2[i], 0, 0)),
            pl.BlockSpec((1, D_FF, D), lambda i, e1, e2, va: (e1[i], 0, 0)),
            pl.BlockSpec((1, D_FF, D), lambda i, e1, e2, va: (e2[i], 0, 0)),
            pl.BlockSpec((1, D), lambda i, e1, e2, va: (0, 0)),
            pl.BlockSpec((1, D), lambda i, e1, e2, va: (0, 0)),
        ],
        out_specs=pl.BlockSpec(memory_space=pl.ANY),
        scratch_shapes=[
            pltpu.VMEM((2, tm, D + LANES), F32),
            pltpu.VMEM((tm, D), F32),
            pltpu.SemaphoreType.DMA((2,)),
            pltpu.SemaphoreType.DMA(()),
        ],
    )
    return pl.pallas_call(
        functools.partial(_moe_kernel, tm=tm, nt=nt),
        grid_spec=grid_spec,
        out_shape=jax.ShapeDtypeStruct((T, D), F32),
        compiler_params=_params(("arbitrary",)),
        name="moe",
    )(tile_e1, tile_e2, tile_rows, src3, src3, xe, w13, w13, w2, w2,
      ln_g.reshape(1, D), ln_b.reshape(1, D))


def _pack_layer_weights(w_in, conv_a, b_if_a, b_f_c):
    a_end = OFF_A + 4 * BW
    w_main = jnp.concatenate(
        [w_in[:, OFF_A:a_end], w_in[:, OFF_G:OFF_G + N_G], w_in[:, OFF_B:OFF_B + BW],
         w_in[:, OFF_C:OFF_C + 2 * BW]], axis=1).astype(BF16)
    wv_t = w_in[:, OFF_C + 2 * BW:OFF_C + 3 * BW].T.astype(BF16)
    n_gate = 2 * A_HEADS + C_HEADS
    wg = jnp.concatenate([w_in[:, a_end:a_end + 2 * A_HEADS], w_in[:, OFF_C + 3 * BW:OFF_C + N_C]], axis=1)
    wg = jnp.pad(wg, ((0, 0), (0, GATE_LANES - n_gate))).astype(BF16)
    bias = jnp.pad(jnp.concatenate([b_if_a, b_f_c]), (0, GATE_LANES - n_gate)).reshape(1, GATE_LANES)
    k_scale = jnp.concatenate([jnp.ones((BW,), F32), jnp.full((BW,), A_HEAD_DIM ** -0.5, F32)])
    cw = jnp.concatenate([conv_a, k_scale[None, :], jnp.zeros((SUBLANES - A_CONV - 1, 2 * BW), F32)], axis=0)
    cw = jnp.pad(cw, ((0, 0), (0, N_MAIN - 2 * BW)))
    return w_main, wv_t, wg, bias, cw


def _forward(x, ln0_g, ln0_b, w_in, conv_a, b_if_a, norm_a, w_pool, pool_scale, b_f_c,
             w_branch, w_out, ln1_g, ln1_b, w_router, b_router, w13, w2, ln2_g, ln2_b):
    B, S, D = x.shape
    T = B * S
    xf, xb = _ln0(x.reshape(T, D), ln0_g, ln0_b)
    perm = (jnp.arange(N_EXPERTS) % N_GROUPS) * EXPERTS_PER_GROUP + jnp.arange(N_EXPERTS) // N_GROUPS
    wr_t = w_router.T[perm]
    br_t = b_router[perm].reshape(N_EXPERTS, 1)
    for l in range(w_in.shape[0]):
        w_main, wv_t, wg, bias, cw = _pack_layer_weights(w_in[l], conv_a[l], b_if_a[l], b_f_c[l])
        proj = _inproj(xb, w_main, cw, S)
        vt = _vt(xb, wv_t, B, S)
        gcol, grow = _gates(xb, wg, bias, B, S)
        ha = _mlstm(proj, gcol, grow, norm_a[l], B, S)
        hb = _pool(proj, w_pool[l].astype(BF16), pool_scale[l], B, S)
        hc = _fox(proj, vt, gcol, grow, B, S)
        x1 = _merge(ha, hb, hc, proj, w_branch[l].astype(BF16), w_out[l].astype(BF16), xf,
                    ln1_g[l], ln1_b[l])
        oi, cnt, xe = _router(x1, wr_t, br_t)
        xf = _moe(xe, oi, cnt, w13[l].astype(BF16), w2[l].astype(BF16), ln2_g[l], ln2_b[l])
        xb = xf.astype(BF16)
    return xf.reshape(B, S, D)


def kernel(x, ln0_g, ln0_b, w_in, conv_a, b_if_a, norm_a, w_pool, pool_scale, b_f_c, w_branch,
           w_out, ln1_g, ln1_b, w_router, b_router, w13, w2, ln2_g, ln2_b):
    return _forward(x, ln0_g, ln0_b, w_in, conv_a, b_if_a, norm_a, w_pool, pool_scale, b_f_c,
                    w_branch, w_out, ln1_g, ln1_b, w_router, b_router, w13, w2, ln2_g, ln2_b)
```

```python
import functools

import jax
import jax.numpy as jnp
from jax import lax
from jax.experimental import pallas as pl
from jax.experimental.pallas import tpu as pltpu

F32 = jnp.float32
BF16 = jnp.bfloat16

D_MODEL = 2048
DEPTH = 2
BW = D_MODEL // 2
A_HEADS = 4
A_HEAD_DIM = BW // A_HEADS
A_CONV = 4
A_CHUNK = 128
B_WINDOWS = (2, 4, 8, 16)
B_GROUP_DIM = BW // len(B_WINDOWS)
C_HEADS = 8
C_HEAD_DIM = BW // C_HEADS
N_BRANCH = 3
OFF_A = 0
N_A = 4 * BW + 2 * A_HEADS
OFF_B = OFF_A + N_A
OFF_C = OFF_B + BW
N_C = 3 * BW + C_HEADS
OFF_G = OFF_C + N_C
N_G = N_BRANCH * D_MODEL
N_EXPERTS = 32
N_GROUPS = 8
EXPERTS_PER_GROUP = N_EXPERTS // N_GROUPS
D_FF = D_MODEL * 3 // 8
ALPHA = (2 * DEPTH) ** 0.25
LN_EPS = 1e-5
LOG2E = 1.4426950408889634

N_MAIN = 2 * BW + N_G + BW + BW + 2 * BW
PB_AQ = 0
PB_AK = 1
PB_GATES = 2
PB_AO = 8
PB_POOL = 9
PB_CQ = 10
PB_CK = 11
VT_A = 0
VT_C = 1
GATE_LANES = 128
N_PAIRS = 6
N_CLASSES = N_GROUPS * N_PAIRS
PAIR_SLOT_A = (0, 0, 0, 1, 1, 3)
PAIR_SLOT_B = (1, 2, 3, 3, 2, 2)

LANES = 128
SUBLANES = 8
VMEM_LIMIT_BYTES = 56 * 1024 * 1024

IN_TN = 1024
IN_RM = 256
IN_UNROLL = 4
EPI_CB = 256
MERGE_TM = 256
ROUTER_TM = 512
MOE_TM = 256
MOE_DMA_UNROLL = 8
FOX_TQ = 256
LN0_TM = 512


def _params(sem, vmem=VMEM_LIMIT_BYTES):
    return pltpu.CompilerParams(dimension_semantics=sem, vmem_limit_bytes=vmem)


def _ln_rows(y, g, b):
    mu = jnp.mean(y, axis=-1, keepdims=True)
    d = y - mu
    var = jnp.mean(d * d, axis=-1, keepdims=True)
    return d * lax.rsqrt(var + LN_EPS) * g + b


def _dot(a, b):
    return jnp.dot(a, b, preferred_element_type=F32)


def _dot_nt(a, b):
    return lax.dot_general(a, b, (((1,), (1,)), ((), ())), preferred_element_type=F32)


def _dot_tn(a, b):
    return lax.dot_general(a, b, (((0,), (0,)), ((), ())), preferred_element_type=F32)


def _ln0_kernel(x_ref, g_ref, b_ref, of_ref, ob_ref):
    y = _ln_rows(x_ref[...], g_ref[...], b_ref[...])
    of_ref[...] = y
    ob_ref[...] = y.astype(BF16)


def _ln0(x2d, g, b):
    T, D = x2d.shape
    tm = LN0_TM
    row = pl.BlockSpec((tm, D), lambda i: (i, 0))
    vec = pl.BlockSpec((1, D), lambda i: (0, 0))
    return pl.pallas_call(
        _ln0_kernel,
        grid=(T // tm,),
        in_specs=[row, vec, vec],
        out_specs=[row, row],
        out_shape=[jax.ShapeDtypeStruct((T, D), F32), jax.ShapeDtypeStruct((T, D), BF16)],
        compiler_params=_params(("parallel",)),
        name="ln0",
    )(x2d, g.reshape(1, D), b.reshape(1, D))


def _inproj_kernel(x_ref, w_ref, cw_ref, o_ref, acc_ref, *, segments, tm, tn):
    j = pl.program_id(1)
    rm = min(IN_RM, tm)

    def chunk(mi, kind):
        r0 = pl.multiple_of(mi * rm, rm)
        if kind == "none":
            acc = _dot(x_ref[pl.ds(r0, rm), :], w_ref[...])
            o_ref[pl.ds(r0, rm), :] = acc.astype(o_ref.dtype)
        elif kind == "sigmoid":
            acc = _dot(x_ref[pl.ds(r0, rm), :], w_ref[...])
            o_ref[pl.ds(r0, rm), :] = jax.nn.sigmoid(acc).astype(o_ref.dtype)
        else:
            @pl.when(mi == 0)
            def _():
                acc_ref[0:SUBLANES, :] = jnp.zeros((SUBLANES, tn), F32)

            for cb in range(tn // EPI_CB):
                cs = slice(cb * EPI_CB, (cb + 1) * EPI_CB)
                acc_ref[SUBLANES:, cs] = _dot(x_ref[pl.ds(r0, rm), :], w_ref[:, cs])
                y = cw_ref[A_CONV - 1:A_CONV, cs] * acc_ref[SUBLANES:SUBLANES + rm, cs]
                for d in range(1, A_CONV):
                    tap = cw_ref[A_CONV - 1 - d:A_CONV - d, cs]
                    y = y + tap * acc_ref[SUBLANES - d:SUBLANES - d + rm, cs]
                y = y * jax.nn.sigmoid(y) * cw_ref[A_CONV:A_CONV + 1, cs]
                o_ref[pl.ds(r0, rm), cs] = y.astype(o_ref.dtype)
            acc_ref[0:SUBLANES, :] = acc_ref[rm:rm + SUBLANES, :]

    for lo, hi, kind in segments:

        @pl.when((j >= lo) & (j < hi))
        def _():
            def body(mi, carry):
                chunk(mi, kind)
                return carry

            lax.fori_loop(0, tm // rm, body, 0, unroll=IN_UNROLL if kind != "conv" else 1)


def _inproj(xb, w_main, cw, S):
    T, K = xb.shape
    N = w_main.shape[1]
    tm, tn = S, IN_TN
    t = lambda c: c // tn
    segments = (
        (t(0), t(PB_GATES * BW), "conv"),
        (t(PB_GATES * BW), t(PB_POOL * BW), "sigmoid"),
        (t(PB_POOL * BW), t(N), "none"),
    )
    return pl.pallas_call(
        functools.partial(_inproj_kernel, segments=segments, tm=tm, tn=tn),
        grid=(T // tm, N // tn),
        in_specs=[
            pl.BlockSpec((tm, K), lambda i, j: (i, 0)),
            pl.BlockSpec((K, tn), lambda i, j: (0, j)),
            pl.BlockSpec((SUBLANES, tn), lambda i, j: (0, j)),
        ],
        out_specs=pl.BlockSpec((tm, tn), lambda i, j: (i, j)),
        out_shape=jax.ShapeDtypeStruct((T, N), BF16),
        scratch_shapes=[pltpu.VMEM((min(IN_RM, tm) + SUBLANES, tn), F32)],
        compiler_params=_params(("parallel", "arbitrary")),
        name="inproj",
    )(xb, w_main, cw)


def _gates_kernel(x_ref, wg_ref, bias_ref, gcol_ref, grow_ref, *, S):
    g = _dot(x_ref[...], wg_ref[...]) + bias_ref[...]
    col = lax.broadcasted_iota(jnp.int32, (1, GATE_LANES), 1)
    n_a = 2 * A_HEADS
    logf = jnp.where((col >= A_HEADS) & (col < n_a + C_HEADS), jax.nn.log_sigmoid(g), 0.0)
    nch = S // A_CHUNK
    cat = jnp.concatenate([logf[c * A_CHUNK:(c + 1) * A_CHUNK, :] for c in range(nch)], axis=1)
    ri = lax.broadcasted_iota(jnp.int32, (A_CHUNK, A_CHUNK), 0)
    ci = lax.broadcasted_iota(jnp.int32, (A_CHUNK, A_CHUNK), 1)
    tri = jnp.where(ri >= ci, 1.0, 0.0).astype(BF16)
    hi = cat.astype(BF16)
    r1 = cat - hi.astype(F32)
    mid = r1.astype(BF16)
    low = (r1 - mid.astype(F32)).astype(BF16)
    within = _dot(tri, hi) + _dot(tri, mid) + _dot(tri, low)
    carry = jnp.zeros((1, GATE_LANES), F32)
    for c in range(nch):
        rows = slice(c * A_CHUNK, (c + 1) * A_CHUNK)
        wc = within[:, c * GATE_LANES:(c + 1) * GATE_LANES]
        glob = wc + carry
        carry = carry + wc[A_CHUNK - 1:A_CHUNK, :]
        gcol_ref[rows, :] = jnp.where(col < A_HEADS, g[rows, :], jnp.where(col < n_a, wc, glob))
    grow_ref[0] = gcol_ref[...].T[0:2 * SUBLANES, :]


def _gates(xb, wg, bias, B, S):
    T, K = xb.shape
    return pl.pallas_call(
        functools.partial(_gates_kernel, S=S),
        grid=(B,),
        in_specs=[
            pl.BlockSpec((S, K), lambda b: (b, 0)),
            pl.BlockSpec((K, GATE_LANES), lambda b: (0, 0)),
            pl.BlockSpec((1, GATE_LANES), lambda b: (0, 0)),
        ],
        out_specs=[
            pl.BlockSpec((S, GATE_LANES), lambda b: (b, 0)),
            pl.BlockSpec((1, 2 * SUBLANES, S), lambda b: (b, 0, 0)),
        ],
        out_shape=[
            jax.ShapeDtypeStruct((T, GATE_LANES), F32),
            jax.ShapeDtypeStruct((B, 2 * SUBLANES, S), F32),
        ],
        compiler_params=_params(("parallel",)),
        name="gates",
    )(xb, wg, bias)


def _mlstm_kernel(q_ref, k_ref, vt_ref, o_ref, gcol_ref, grow_ref, na_ref, out_ref, st_ref, m_ref):
    L = A_CHUNK
    dh = A_HEAD_DIM

    @pl.when(pl.program_id(1) == 0)
    def _():
        st_ref[...] = jnp.zeros_like(st_ref)
        m_ref[...] = jnp.zeros_like(m_ref)

    si = lax.broadcasted_iota(jnp.int32, (L, L), 0)
    ti = lax.broadcasted_iota(jnp.int32, (L, L), 1)
    causal = si <= ti
    ones_row = jnp.where(lax.broadcasted_iota(jnp.int32, (SUBLANES, L), 0) == 0, 1.0, 0.0)
    gcol = gcol_ref[...]
    grow = grow_ref[0]
    for h in range(A_HEADS):
        cs = slice(h * dh, (h + 1) * dh)
        q = q_ref[:, cs]
        k = k_ref[:, cs]
        vt = vt_ref[0, cs, :]
        i_col = gcol[:, h:h + 1]
        b_col = gcol[:, A_HEADS + h:A_HEADS + h + 1]
        i_row = grow[h:h + 1, :]
        b_row = grow[A_HEADS + h:A_HEADS + h + 1, :]
        g = b_row[:, L - 1:L]
        m = m_ref[h]
        dmat = jnp.where(causal, b_row - b_col + i_col, -jnp.inf)
        inter = b_row + m
        m_row = jnp.maximum(inter, jnp.max(dmat, axis=0, keepdims=True))
        w_intra = jnp.exp(dmat - m_row)
        w_inter = jnp.exp(inter - m_row)
        s = _dot_nt(k, q) * w_intra
        state = st_ref[h]
        read = _dot_nt(state.astype(BF16), q)
        num = w_inter * read[0:dh, :] + _dot(vt, s.astype(BF16))
        den = w_inter * read[dh:dh + 1, :] + jnp.sum(s, axis=0, keepdims=True)
        hh = num / jnp.maximum(jnp.abs(den), jnp.exp(-m_row))
        a_row = g - b_row + i_row
        m_new = jnp.maximum(g + m, jnp.max(a_row, axis=1, keepdims=True))
        decay = jnp.exp(g + m - m_new)
        w_row = jnp.exp(a_row - m_new)
        lhs = jnp.concatenate([vt.astype(F32) * w_row, ones_row * w_row], axis=0)
        st_ref[h] = decay * state + _dot(lhs.astype(BF16), k)
        m_ref[h] = m_new
        mu = jnp.mean(hh, axis=0, keepdims=True)
        d = hh - mu
        var = jnp.mean(d * d, axis=0, keepdims=True)
        hn = (d * lax.rsqrt(var + LN_EPS)).T * na_ref[:, cs] * o_ref[:, cs].astype(F32)
        out_ref[:, cs] = hn.astype(out_ref.dtype)


def _mlstm(proj, vt, gcol, grow, norm_a, B, S):
    T = proj.shape[0]
    L = A_CHUNK
    nc = S // L
    blk = lambda cb: pl.BlockSpec((L, BW), lambda b, c, cb=cb: (b * nc + c, cb))
    return pl.pallas_call(
        _mlstm_kernel,
        grid=(B, nc),
        in_specs=[
            blk(PB_AQ), blk(PB_AK),
            pl.BlockSpec((1, BW, L), lambda b, c: (b, VT_A, c)),
            blk(PB_AO),
            pl.BlockSpec((L, GATE_LANES), lambda b, c: (b * nc + c, 0)),
            pl.BlockSpec((1, 2 * SUBLANES, L), lambda b, c: (b, 0, c)),
            pl.BlockSpec((1, BW), lambda b, c: (0, 0)),
        ],
        out_specs=pl.BlockSpec((L, BW), lambda b, c: (b * nc + c, 0)),
        out_shape=jax.ShapeDtypeStruct((T, BW), BF16),
        scratch_shapes=[
            pltpu.VMEM((A_HEADS, A_HEAD_DIM + SUBLANES, A_HEAD_DIM), F32),
            pltpu.VMEM((A_HEADS, 1, 1), F32),
        ],
        compiler_params=_params(("parallel", "arbitrary")),
        name="mlstm",
    )(proj, proj, vt, proj, gcol, grow, norm_a.reshape(1, BW))


def _pool_kernel(u_ref, wp_ref, ps_ref, out_ref, *, S):
    t = lax.broadcasted_iota(jnp.int32, (S, 1), 0)
    gd = B_GROUP_DIM
    for gi, w in enumerate(B_WINDOWS):
        cs = slice(gi * gd, (gi + 1) * gd)
        u = u_ref[:, cs].astype(F32)
        s = u
        k = 1
        while k < w:
            s = s + jnp.where(t >= k, pltpu.roll(s, k, axis=0), 0.0)
            k *= 2
        cnt = jnp.minimum(t + 1, w).astype(F32)
        pooled = s / cnt - u
        y = _dot(pooled.astype(BF16), wp_ref[gi]) * ps_ref[:, cs]
        out_ref[:, cs] = y.astype(out_ref.dtype)


def _pool(proj, w_pool, pool_scale, B, S):
    T = proj.shape[0]
    G = len(B_WINDOWS)
    return pl.pallas_call(
        functools.partial(_pool_kernel, S=S),
        grid=(B,),
        in_specs=[
            pl.BlockSpec((S, BW), lambda b: (b, PB_POOL)),
            pl.BlockSpec((G, B_GROUP_DIM, B_GROUP_DIM), lambda b: (0, 0, 0)),
            pl.BlockSpec((1, BW), lambda b: (0, 0)),
        ],
        out_specs=pl.BlockSpec((S, BW), lambda b: (b, 0)),
        out_shape=jax.ShapeDtypeStruct((T, BW), BF16),
        compiler_params=_params(("parallel",)),
        name="pool",
    )(proj, w_pool, pool_scale.reshape(1, BW))


def _vt_kernel(x_ref, wt_ref, o_ref, *, S):
    rm = min(IN_RM, S)

    def body(mi, carry):
        r0 = pl.multiple_of(mi * rm, rm)
        o_ref[0, :, pl.ds(r0, rm)] = _dot_nt(wt_ref[...], x_ref[pl.ds(r0, rm), :]).astype(o_ref.dtype)
        return carry

    lax.fori_loop(0, S // rm, body, 0, unroll=2)


def _vt(xb, wv_t, B, S):
    K = xb.shape[1]
    nb = wv_t.shape[0] // BW
    return pl.pallas_call(
        functools.partial(_vt_kernel, S=S),
        grid=(B, nb),
        in_specs=[
            pl.BlockSpec((S, K), lambda b, j: (b, 0)),
            pl.BlockSpec((BW, K), lambda b, j: (j, 0)),
        ],
        out_specs=pl.BlockSpec((1, BW, S), lambda b, j: (b, j, 0)),
        out_shape=jax.ShapeDtypeStruct((B, nb * BW, S), BF16),
        compiler_params=_params(("parallel", "arbitrary")),
        name="vt",
    )(xb, wv_t)


def _fox_kernel(q_ref, k_ref, vt_ref, gcol_ref, grow_ref, out_ref, m_ref, l_ref, a_ref, acc_ref,
                s_ref, p_ref, *, tq):
    qi = pl.program_id(1)
    dh = C_HEAD_DIM
    scale = dh ** -0.5
    n_a = 2 * A_HEADS
    m_ref[...] = jnp.full(m_ref.shape, -jnp.inf, F32)
    l_ref[...] = jnp.zeros(l_ref.shape, F32)
    acc_ref[...] = jnp.zeros(acc_ref.shape, F32)

    def block(kj, masked):
        k0 = pl.multiple_of(kj * tq, tq)
        for h in range(C_HEADS):
            cs = slice(h * dh, (h + 1) * dh)
            s_ref[h] = _dot_nt(k_ref[pl.ds(k0, tq), cs], q_ref[:, cs])
        for h in range(C_HEADS):
            c_key = gcol_ref[pl.ds(k0, tq), n_a + h:n_a + h + 1]
            c_qry = grow_ref[0, n_a + h:n_a + h + 1, :]
            lg = s_ref[h] * (scale * LOG2E) + (c_qry * LOG2E - c_key * LOG2E)
            if masked:
                si = lax.broadcasted_iota(jnp.int32, (tq, tq), 0)
                ti = lax.broadcasted_iota(jnp.int32, (tq, tq), 1)
                lg = jnp.where(si <= ti, lg, -jnp.inf)
            s_ref[h] = lg
            m = m_ref[h]
            m_new = jnp.maximum(m, jnp.max(lg, axis=0, keepdims=True))
            a_ref[h] = jnp.exp2(m - m_new)
            m_ref[h] = m_new
        for h in range(C_HEADS):
            p = jnp.exp2(s_ref[h] - m_ref[h])
            l_ref[h] = a_ref[h] * l_ref[h] + jnp.sum(p, axis=0, keepdims=True)
            p_ref[h] = p.astype(BF16)
        for h in range(C_HEADS):
            cs = slice(h * dh, (h + 1) * dh)
            vt = vt_ref[0, cs, pl.ds(k0, tq)]
            acc_ref[h] = a_ref[h] * acc_ref[h] + _dot(vt, p_ref[h])

    def body(kj, carry):
        block(kj, False)
        return carry

    lax.fori_loop(0, qi, body, 0)
    block(qi, True)
    for h in range(C_HEADS):
        cs = slice(h * dh, (h + 1) * dh)
        out_ref[:, cs] = (acc_ref[h] / l_ref[h]).T.astype(out_ref.dtype)


def _fox(proj, vt, gcol, grow, B, S):
    T = proj.shape[0]
    tq = min(FOX_TQ, S)
    nq = S // tq
    return pl.pallas_call(
        functools.partial(_fox_kernel, tq=tq),
        grid=(B, nq),
        in_specs=[
            pl.BlockSpec((tq, BW), lambda b, i: (b * nq + i, PB_CQ)),
            pl.BlockSpec((S, BW), lambda b, i: (b, PB_CK)),
            pl.BlockSpec((1, BW, S), lambda b, i: (b, VT_C, 0)),
            pl.BlockSpec((S, GATE_LANES), lambda b, i: (b, 0)),
            pl.BlockSpec((1, 2 * SUBLANES, tq), lambda b, i: (b, 0, i)),
        ],
        out_specs=pl.BlockSpec((tq, BW), lambda b, i: (b * nq + i, 0)),
        out_shape=jax.ShapeDtypeStruct((T, BW), BF16),
        scratch_shapes=[
            pltpu.VMEM((C_HEADS, 1, tq), F32),
            pltpu.VMEM((C_HEADS, 1, tq), F32),
            pltpu.VMEM((C_HEADS, 1, tq), F32),
            pltpu.VMEM((C_HEADS, C_HEAD_DIM, tq), F32),
            pltpu.VMEM((C_HEADS, tq, tq), F32),
            pltpu.VMEM((C_HEADS, tq, tq), BF16),
        ],
        compiler_params=_params(("parallel", "arbitrary")),
        name="fox",
    )(proj, proj, vt, gcol, grow)


def _merge_kernel(ha_ref, hb_ref, hc_ref, g0_ref, g1_ref, g2_ref, wb_ref, wo_ref, x_ref,
                  lg_ref, lb_ref, xo_ref):
    merged = g0_ref[...].astype(F32) * _dot(ha_ref[...], wb_ref[0])
    merged = merged + g1_ref[...].astype(F32) * _dot(hb_ref[...], wb_ref[1])
    merged = merged + g2_ref[...].astype(F32) * _dot(hc_ref[...], wb_ref[2])
    h = _dot(merged.astype(BF16), wo_ref[...])
    xo_ref[...] = _ln_rows(ALPHA * x_ref[...] + h, lg_ref[...], lb_ref[...])


def _merge(ha, hb, hc, proj, w_branch, w_out, x, ln_g, ln_b):
    T, D = x.shape
    tm = MERGE_TM
    hblk = pl.BlockSpec((tm, BW), lambda i: (i, 0))
    gblk = lambda n: pl.BlockSpec((tm, D), lambda i, n=n: (i, PB_GATES * BW // D + n))
    const = lambda shape: pl.BlockSpec(shape, lambda i: (0,) * len(shape), pipeline_mode=pl.Buffered(1))
    row = pl.BlockSpec((tm, D), lambda i: (i, 0))
    return pl.pallas_call(
        _merge_kernel,
        grid=(T // tm,),
        in_specs=[hblk, hblk, hblk, gblk(0), gblk(1), gblk(2),
                  const((N_BRANCH, BW, D)), const((D, D)), row, const((1, D)), const((1, D))],
        out_specs=row,
        out_shape=jax.ShapeDtypeStruct((T, D), F32),
        compiler_params=_params(("parallel",)),
        name="merge",
    )(ha, hb, hc, proj, proj, proj, w_branch, w_out, x, ln_g.reshape(1, D), ln_b.reshape(1, D))


def _router_kernel(x_ref, wr_ref, br_ref, oi_ref, cnt_ref, xe_ref):
    tm = x_ref.shape[0]
    ng, ne = N_GROUPS, EXPERTS_PER_GROUP
    def split(a):
        hi = a.astype(BF16)
        return hi, (a - hi.astype(F32)).astype(BF16)

    x_hi, x_lo = split(x_ref[...])
    w_hi, w_lo = split(wr_ref[...])
    lt = _dot_nt(w_hi, x_hi) + (_dot_nt(w_hi, x_lo) + _dot_nt(w_lo, x_hi))
    logit = [lt[ng * j:ng * (j + 1), :] for j in range(ne)]
    mx = functools.reduce(jnp.maximum, logit)
    mx = jnp.max(mx, axis=0, keepdims=True)
    ex = [jnp.exp(a - mx) for a in logit]
    den = jnp.sum(functools.reduce(jnp.add, ex), axis=0, keepdims=True)
    prob = [e / den for e in ex]
    sel = [prob[j] + br_ref[ng * j:ng * (j + 1), :] for j in range(ne)]

    def first_argmax(vals):
        best = functools.reduce(jnp.maximum, vals)
        idx = jnp.full(best.shape, float(ne - 1), F32)
        for j in range(ne - 2, -1, -1):
            idx = jnp.where(vals[j] == best, float(j), idx)
        return best, idx

    def pick(vals, idx):
        out = vals[ne - 1]
        for j in range(ne - 2, -1, -1):
            out = jnp.where(idx == float(j), vals[j], out)
        return out

    m1, i1 = first_argmax(sel)
    rest = [jnp.where(i1 == float(j), -jnp.inf, sel[j]) for j in range(ne)]
    m2, i2 = first_argmax(rest)
    score = m1 + m2
    gi = lax.broadcasted_iota(jnp.int32, (ng, tm), 0).astype(F32)
    gidx = jnp.min(jnp.where(score == jnp.max(score, axis=0, keepdims=True), gi, float(ng)),
                   axis=0, keepdims=True)
    chosen = gi == gidx
    take = lambda a: jnp.sum(jnp.where(chosen, a, 0.0), axis=0, keepdims=True)
    e1, e2 = take(i1), take(i2)
    p1, p2 = take(pick(prob, i1)), take(pick(prob, i2))
    psum = p1 + p2
    p1, p2 = p1 / psum, p2 / psum
    lo, hi = jnp.minimum(e1, e2), jnp.maximum(e1, e2)
    last = float(N_PAIRS - 1)
    pair = jnp.where(lo == 0.0, hi - 1.0, jnp.where(lo == 1.0, 6.0 - hi, last))
    w_lo = jnp.where(e1 < e2, p1, p2)
    w_hi = jnp.where(e1 < e2, p2, p1)
    w_a = jnp.where(pair == last, w_hi, w_lo)
    w_b = jnp.where(pair == last, w_lo, w_hi)
    cls = gidx * N_PAIRS + pair
    @pl.when(pl.program_id(0) == 0)
    def _():
        cnt_ref[...] = jnp.zeros_like(cnt_ref)

    onehot = lax.broadcasted_iota(jnp.int32, (N_CLASSES, tm), 0).astype(F32) == cls
    jr = lax.broadcasted_iota(jnp.int32, (tm, tm), 0)
    tc = lax.broadcasted_iota(jnp.int32, (tm, tm), 1)
    upper = jnp.where(jr <= tc, 1.0, 0.0).astype(BF16)
    cum = _dot(jnp.where(onehot, 1.0, 0.0).astype(BF16), upper)
    before = cnt_ref[:, 0:1]
    rank = jnp.sum(jnp.where(onehot, cum - 1.0 + before, 0.0), axis=0, keepdims=True)
    cnt_ref[...] = cnt_ref[...] + cum[:, tm - 1:tm]
    zero = jnp.zeros((1, tm), F32)
    rows_i = [cls, gidx * ne + lo, gidx * ne + hi, rank] + [zero] * (SUBLANES - 4)
    oi_ref[...] = jnp.concatenate(rows_i, axis=0).astype(jnp.int32)
    d = x_ref.shape[1]
    xe_ref[:, 0:d] = x_ref[...]
    extra = jnp.concatenate([w_a, w_b, jnp.zeros((LANES - 2, tm), F32)], axis=0)
    xe_ref[:, d:] = extra.T


def _router(x, wr_t, br_t):
    T, D = x.shape
    tm = min(ROUTER_TM, T)
    return pl.pallas_call(
        _router_kernel,
        grid=(T // tm,),
        in_specs=[
            pl.BlockSpec((tm, D), lambda i: (i, 0)),
            pl.BlockSpec((N_EXPERTS, D), lambda i: (0, 0)),
            pl.BlockSpec((N_EXPERTS, 1), lambda i: (0, 0)),
        ],
        out_specs=[
            pl.BlockSpec((SUBLANES, tm), lambda i: (0, i)),
            pl.BlockSpec((N_CLASSES, LANES), lambda i: (0, 0)),
            pl.BlockSpec((tm, D + LANES), lambda i: (i, 0)),
        ],
        out_shape=[
            jax.ShapeDtypeStruct((SUBLANES, T), jnp.int32),
            jax.ShapeDtypeStruct((N_CLASSES, LANES), F32),
            jax.ShapeDtypeStruct((T, D + LANES), F32),
        ],
        compiler_params=_params(("arbitrary",)),
        name="router",
    )(x, wr_t, br_t)


def _moe_kernel(ea_ref, eb_ref, nrows_ref, src_ref, srcn_ref, x_hbm, w13a_ref, w13b_ref,
                w2a_ref, w2b_ref, lg_ref, lb_ref, out_hbm, xbuf, obuf, sem_in, sem_out, *, tm, nt):
    i = pl.program_id(0)
    slot = i % 2
    n_cur = nrows_ref[i]
    n_next = nrows_ref[jnp.minimum(i + 1, nt - 1)]
    n_prev = nrows_ref[jnp.maximum(i - 1, 0)]
    has_next = (i + 1 < nt) & (n_next > 0)
    has_prev = (i > 0) & (n_prev > 0)

    def gather(idx_ref, s):
        def start(r, c):
            tok = jnp.maximum(idx_ref[0, 0, r], 0)
            pltpu.make_async_copy(x_hbm.at[pl.ds(tok, 1)], xbuf.at[s, pl.ds(r, 1)],
                                  sem_in.at[s]).start()
            return c

        lax.fori_loop(0, tm, start, 0, unroll=MOE_DMA_UNROLL)

    def wait_gather(s):
        pltpu.make_async_copy(x_hbm.at[pl.ds(0, tm)], xbuf.at[s], sem_in.at[s]).wait()

    def wait_scatter(n):
        p = tm
        while p >= 1:
            @pl.when((n & p) != 0)
            def _(p=p):
                pltpu.make_async_copy(obuf.at[pl.ds(0, p)], out_hbm.at[pl.ds(0, p)], sem_out).wait()

            p //= 2

    @pl.when((i == 0) & (n_cur > 0))
    def _():
        gather(src_ref, 0)

    @pl.when(n_cur > 0)
    def _():
        wait_gather(slot)

    @pl.when(has_next)
    def _():
        gather(srcn_ref, 1 - slot)

    @pl.when(n_cur > 0)
    def _():
        d = out_hbm.shape[1]
        x = xbuf[slot, :, 0:d]
        w_a = xbuf[slot, :, d:d + 1]
        w_b = xbuf[slot, :, d + 1:d + 2]
        xb = x.astype(BF16)

        def expert(w13_ref, w2_ref):
            h = _dot(xb, w13_ref[0])
            a = jax.nn.silu(h[:, :D_FF]) * h[:, D_FF:]
            return _dot(a.astype(BF16), w2_ref[0])

        y = w_a * expert(w13a_ref, w2a_ref)
        y = y + w_b * expert(w13b_ref, w2b_ref)
        z = _ln_rows(ALPHA * x + y, lg_ref[...], lb_ref[...])

        @pl.when(has_prev)
        def _():
            wait_scatter(n_prev)

        obuf[...] = z

        def start_out(r, c):
            tok = src_ref[0, 0, r]
            pltpu.make_async_copy(obuf.at[pl.ds(r, 1)], out_hbm.at[pl.ds(tok, 1)], sem_out).start()
            return c

        @pl.when(n_cur == tm)
        def _():
            lax.fori_loop(0, tm, start_out, 0, unroll=MOE_DMA_UNROLL)

        @pl.when(n_cur < tm)
        def _():
            lax.fori_loop(0, n_cur, start_out, 0)

    @pl.when((n_cur == 0) & has_prev)
    def _():
        wait_scatter(n_prev)

    @pl.when((i == nt - 1) & (n_cur > 0))
    def _():
        wait_scatter(n_cur)


def _moe(xe, oi, cnt, w13, w2, ln_g, ln_b):
    T = xe.shape[0]
    D = D_MODEL
    tm = MOE_TM
    npad = T + N_CLASSES * tm
    nt = npad // tm
    cls, rank = oi[0], oi[3]
    counts = cnt[:, 0].astype(jnp.int32)
    ptiles = (counts + tm - 1) // tm
    tile_end = jnp.cumsum(ptiles)
    tile_off = tile_end - ptiles
    in_class = cls[:, None] == jnp.arange(N_CLASSES, dtype=jnp.int32)[None, :]
    pos = jnp.sum(jnp.where(in_class, tile_off[None, :], 0), axis=1) * tm + rank
    src = jnp.full((npad,), -1, jnp.int32).at[pos].set(jnp.arange(T, dtype=jnp.int32))
    tile_id = jnp.arange(nt, dtype=jnp.int32)
    n_used = tile_end[-1]
    tile_cls = jnp.searchsorted(tile_end, jnp.minimum(tile_id, n_used - 1), side="right").astype(jnp.int32)
    tile_rows = jnp.clip(counts[tile_cls] - (tile_id - tile_off[tile_cls]) * tm, 0, tm)
    tile_rows = jnp.where(tile_id < n_used, tile_rows, 0).astype(jnp.int32)
    pair = tile_cls % N_PAIRS
    tile_e1 = (tile_cls // N_PAIRS) * EXPERTS_PER_GROUP + jnp.array(PAIR_SLOT_A, jnp.int32)[pair]
    tile_e2 = (tile_cls // N_PAIRS) * EXPERTS_PER_GROUP + jnp.array(PAIR_SLOT_B, jnp.int32)[pair]
    src3 = src.reshape(nt, 1, tm)

    F2 = 2 * D_FF
    grid_spec = pltpu.PrefetchScalarGridSpec(
        num_scalar_prefetch=3,
        grid=(nt,),
        in_specs=[
            pl.BlockSpec((1, 1, tm), lambda i, e1, e2, va: (i, 0, 0), memory_space=pltpu.SMEM),
            pl.BlockSpec((1, 1, tm), lambda i, e1, e2, va: (jnp.minimum(i + 1, nt - 1), 0, 0),
                         memory_space=pltpu.SMEM),
            pl.BlockSpec(memory_space=pl.ANY),
            pl.BlockSpec((1, D, F2), lambda i, e1, e2, va: (e1[i], 0, 0)),
            pl.BlockSpec((1, D, F2), lambda i, e1, e2, va: (e2[i], 0, 0)),
            pl.BlockSpec((1, D_FF, D), lambda i, e1, e2, va: (e1[i], 0, 0)),
            pl.BlockSpec((1, D_FF, D), lambda i, e1, e2, va: (e2[i], 0, 0)),
            pl.BlockSpec((1, D), lambda i, e1, e2, va: (0, 0)),
            pl.BlockSpec((1, D), lambda i, e1, e2, va: (0, 0)),
        ],
        out_specs=pl.BlockSpec(memory_space=pl.ANY),
        scratch_shapes=[
            pltpu.VMEM((2, tm, D + LANES), F32),
            pltpu.VMEM((tm, D), F32),
            pltpu.SemaphoreType.DMA((2,)),
            pltpu.SemaphoreType.DMA(()),
        ],
    )
    return pl.pallas_call(
        functools.partial(_moe_kernel, tm=tm, nt=nt),
        grid_spec=grid_spec,
        out_shape=jax.ShapeDtypeStruct((T, D), F32),
        compiler_params=_params(("arbitrary",)),
        name="moe",
    )(tile_e1, tile_e2, tile_rows, src3, src3, xe, w13, w13, w2, w2,
      ln_g.reshape(1, D), ln_b.reshape(1, D))


def _pack_layer_weights(w_in, conv_a, b_if_a, b_f_c):
    a_end = OFF_A + 4 * BW
    w_main = jnp.concatenate(
        [w_in[:, OFF_A:OFF_A + 2 * BW], w_in[:, OFF_G:OFF_G + N_G],
         w_in[:, OFF_A + 3 * BW:a_end], w_in[:, OFF_B:OFF_B + BW],
         w_in[:, OFF_C:OFF_C + 2 * BW]], axis=1).astype(BF16)
    wv_t = jnp.concatenate(
        [w_in[:, OFF_A + 2 * BW:OFF_A + 3 * BW], w_in[:, OFF_C + 2 * BW:OFF_C + 3 * BW]],
        axis=1).T.astype(BF16)
    n_gate = 2 * A_HEADS + C_HEADS
    wg = jnp.concatenate([w_in[:, a_end:a_end + 2 * A_HEADS], w_in[:, OFF_C + 3 * BW:OFF_C + N_C]], axis=1)
    wg = jnp.pad(wg, ((0, 0), (0, GATE_LANES - n_gate))).astype(BF16)
    bias = jnp.pad(jnp.concatenate([b_if_a, b_f_c]), (0, GATE_LANES - n_gate)).reshape(1, GATE_LANES)
    k_scale = jnp.concatenate([jnp.ones((BW,), F32), jnp.full((BW,), A_HEAD_DIM ** -0.5, F32)])
    cw = jnp.concatenate([conv_a, k_scale[None, :], jnp.zeros((SUBLANES - A_CONV - 1, 2 * BW), F32)], axis=0)
    cw = jnp.pad(cw, ((0, 0), (0, N_MAIN - 2 * BW)))
    return w_main, wv_t, wg, bias, cw


def _forward(x, ln0_g, ln0_b, w_in, conv_a, b_if_a, norm_a, w_pool, pool_scale, b_f_c,
             w_branch, w_out, ln1_g, ln1_b, w_router, b_router, w13, w2, ln2_g, ln2_b):
    B, S, D = x.shape
    T = B * S
    xf, xb = _ln0(x.reshape(T, D), ln0_g, ln0_b)
    perm = (jnp.arange(N_EXPERTS) % N_GROUPS) * EXPERTS_PER_GROUP + jnp.arange(N_EXPERTS) // N_GROUPS
    wr_t = w_router.T[perm]
    br_t = b_router[perm].reshape(N_EXPERTS, 1)
    for l in range(w_in.shape[0]):
        w_main, wv_t, wg, bias, cw = _pack_layer_weights(w_in[l], conv_a[l], b_if_a[l], b_f_c[l])
        proj = _inproj(xb, w_main, cw, S)
        vt = _vt(xb, wv_t, B, S)
        gcol, grow = _gates(xb, wg, bias, B, S)
        ha = _mlstm(proj, vt, gcol, grow, norm_a[l], B, S)
        hb = _pool(proj, w_pool[l].astype(BF16), pool_scale[l], B, S)
        hc = _fox(proj, vt, gcol, grow, B, S)
        x1 = _merge(ha, hb, hc, proj, w_branch[l].astype(BF16), w_out[l].astype(BF16), xf,
                    ln1_g[l], ln1_b[l])
        oi, cnt, xe = _router(x1, wr_t, br_t)
        xf = _moe(xe, oi, cnt, w13[l].astype(BF16), w2[l].astype(BF16), ln2_g[l], ln2_b[l])
        xb = xf.astype(BF16)
    return xf.reshape(B, S, D)


def kernel(x, ln0_g, ln0_b, w_in, conv_a, b_if_a, norm_a, w_pool, pool_scale, b_f_c, w_branch,
           w_out, ln1_g, ln1_b, w_router, b_router, w13, w2, ln2_g, ln2_b):
    return _forward(x, ln0_g, ln0_b, w_in, conv_a, b_if_a, norm_a, w_pool, pool_scale, b_f_c,
                    w_branch, w_out, ln1_g, ln1_b, w_router, b_router, w13, w2, ln2_g, ln2_b)
```

```python
import functools

import jax
import jax.numpy as jnp
from jax import lax
from jax.experimental import pallas as pl
from jax.experimental.pallas import tpu as pltpu

F32 = jnp.float32
BF16 = jnp.bfloat16

D_MODEL = 2048
DEPTH = 2
BW = D_MODEL // 2
A_HEADS = 4
A_HEAD_DIM = BW // A_HEADS
A_CONV = 4
A_CHUNK = 128
B_WINDOWS = (2, 4, 8, 16)
B_GROUP_DIM = BW // len(B_WINDOWS)
C_HEADS = 8
C_HEAD_DIM = BW // C_HEADS
N_BRANCH = 3
OFF_A = 0
N_A = 4 * BW + 2 * A_HEADS
OFF_B = OFF_A + N_A
OFF_C = OFF_B + BW
N_C = 3 * BW + C_HEADS
OFF_G = OFF_C + N_C
N_G = N_BRANCH * D_MODEL
N_EXPERTS = 32
N_GROUPS = 8
EXPERTS_PER_GROUP = N_EXPERTS // N_GROUPS
D_FF = D_MODEL * 3 // 8
ALPHA = (2 * DEPTH) ** 0.25
LN_EPS = 1e-5
LOG2E = 1.4426950408889634

N_MAIN = 2 * BW + N_G + BW + BW + 2 * BW
PB_AQ = 0
PB_AK = 1
PB_GATES = 2
PB_AO = 8
PB_POOL = 9
PB_CQ = 10
PB_CK = 11
VT_A = 0
VT_C = 1
GATE_LANES = 128
N_PAIRS = 6
N_CLASSES = N_GROUPS * N_PAIRS
PAIR_SLOT_A = (0, 0, 0, 1, 1, 3)
PAIR_SLOT_B = (1, 2, 3, 3, 2, 2)

LANES = 128
SUBLANES = 8
VMEM_LIMIT_BYTES = 56 * 1024 * 1024

IN_TN = 1024
IN_RM = 256
IN_UNROLL = 4
EPI_CB = 256
MERGE_TM = 256
MOE_TM = 256
MOE_DMA_UNROLL = 8
FOX_TQ = 256
ROUTER_TM = 512
LN0_TM = 512


def _params(sem, vmem=VMEM_LIMIT_BYTES):
    return pltpu.CompilerParams(dimension_semantics=sem, vmem_limit_bytes=vmem)


def _ln_rows(y, g, b):
    mu = jnp.mean(y, axis=-1, keepdims=True)
    d = y - mu
    var = jnp.mean(d * d, axis=-1, keepdims=True)
    return d * lax.rsqrt(var + LN_EPS) * g + b


def _dot(a, b):
    return jnp.dot(a, b, preferred_element_type=F32)


def _dot_nt(a, b):
    return lax.dot_general(a, b, (((1,), (1,)), ((), ())), preferred_element_type=F32)


def _ln0_kernel(x_ref, g_ref, b_ref, of_ref, ob_ref):
    y = _ln_rows(x_ref[...], g_ref[...], b_ref[...])
    of_ref[...] = y
    ob_ref[...] = y.astype(BF16)


def _ln0(x2d, g, b):
    T, D = x2d.shape
    tm = LN0_TM
    row = pl.BlockSpec((tm, D), lambda i: (i, 0))
    vec = pl.BlockSpec((1, D), lambda i: (0, 0))
    return pl.pallas_call(
        _ln0_kernel,
        grid=(T // tm,),
        in_specs=[row, vec, vec],
        out_specs=[row, row],
        out_shape=[jax.ShapeDtypeStruct((T, D), F32), jax.ShapeDtypeStruct((T, D), BF16)],
        compiler_params=_params(("parallel",)),
        name="ln0",
    )(x2d, g.reshape(1, D), b.reshape(1, D))


def _inproj_kernel(x_ref, w_ref, cw_ref, o_ref, acc_ref, *, segments, tm, tn):
    j = pl.program_id(1)
    rm = min(IN_RM, tm)

    def chunk(mi, kind):
        r0 = pl.multiple_of(mi * rm, rm)
        if kind == "none":
            acc = _dot(x_ref[pl.ds(r0, rm), :], w_ref[...])
            o_ref[pl.ds(r0, rm), :] = acc.astype(o_ref.dtype)
        elif kind == "sigmoid":
            acc = _dot(x_ref[pl.ds(r0, rm), :], w_ref[...])
            o_ref[pl.ds(r0, rm), :] = jax.nn.sigmoid(acc).astype(o_ref.dtype)
        else:
            @pl.when(mi == 0)
            def _():
                acc_ref[0:SUBLANES, :] = jnp.zeros((SUBLANES, tn), F32)

            for cb in range(tn // EPI_CB):
                cs = slice(cb * EPI_CB, (cb + 1) * EPI_CB)
                acc_ref[SUBLANES:, cs] = _dot(x_ref[pl.ds(r0, rm), :], w_ref[:, cs])
                y = cw_ref[A_CONV - 1:A_CONV, cs] * acc_ref[SUBLANES:SUBLANES + rm, cs]
                for d in range(1, A_CONV):
                    tap = cw_ref[A_CONV - 1 - d:A_CONV - d, cs]
                    y = y + tap * acc_ref[SUBLANES - d:SUBLANES - d + rm, cs]
                y = y * jax.nn.sigmoid(y) * cw_ref[A_CONV:A_CONV + 1, cs]
                o_ref[pl.ds(r0, rm), cs] = y.astype(o_ref.dtype)
            acc_ref[0:SUBLANES, :] = acc_ref[rm:rm + SUBLANES, :]

    for lo, hi, kind in segments:

        @pl.when((j >= lo) & (j < hi))
        def _():
            def body(mi, carry):
                chunk(mi, kind)
                return carry

            lax.fori_loop(0, tm // rm, body, 0, unroll=IN_UNROLL if kind != "conv" else 1)


def _inproj(xb, w_main, cw, S):
    T, K = xb.shape
    N = w_main.shape[1]
    tm, tn = S, IN_TN
    t = lambda c: c // tn
    segments = (
        (t(0), t(PB_GATES * BW), "conv"),
        (t(PB_GATES * BW), t(PB_POOL * BW), "sigmoid"),
        (t(PB_POOL * BW), t(N), "none"),
    )
    return pl.pallas_call(
        functools.partial(_inproj_kernel, segments=segments, tm=tm, tn=tn),
        grid=(T // tm, N // tn),
        in_specs=[
            pl.BlockSpec((tm, K), lambda i, j: (i, 0)),
            pl.BlockSpec((K, tn), lambda i, j: (0, j)),
            pl.BlockSpec((SUBLANES, tn), lambda i, j: (0, j)),
        ],
        out_specs=pl.BlockSpec((tm, tn), lambda i, j: (i, j)),
        out_shape=jax.ShapeDtypeStruct((T, N), BF16),
        scratch_shapes=[pltpu.VMEM((min(IN_RM, tm) + SUBLANES, tn), F32)],
        compiler_params=_params(("parallel", "arbitrary")),
        name="inproj",
    )(xb, w_main, cw)


def _gates_kernel(x_ref, wg_ref, bias_ref, gcol_ref, grow_ref, *, S):
    g = _dot(x_ref[...], wg_ref[...]) + bias_ref[...]
    col = lax.broadcasted_iota(jnp.int32, (1, GATE_LANES), 1)
    n_a = 2 * A_HEADS
    logf = jnp.where((col >= A_HEADS) & (col < n_a + C_HEADS), jax.nn.log_sigmoid(g), 0.0)
    nch = S // A_CHUNK
    cat = jnp.concatenate([logf[c * A_CHUNK:(c + 1) * A_CHUNK, :] for c in range(nch)], axis=1)
    ri = lax.broadcasted_iota(jnp.int32, (A_CHUNK, A_CHUNK), 0)
    ci = lax.broadcasted_iota(jnp.int32, (A_CHUNK, A_CHUNK), 1)
    tri = jnp.where(ri >= ci, 1.0, 0.0).astype(BF16)
    hi = cat.astype(BF16)
    r1 = cat - hi.astype(F32)
    mid = r1.astype(BF16)
    low = (r1 - mid.astype(F32)).astype(BF16)
    within = _dot(tri, hi) + _dot(tri, mid) + _dot(tri, low)
    carry = jnp.zeros((1, GATE_LANES), F32)
    for c in range(nch):
        rows = slice(c * A_CHUNK, (c + 1) * A_CHUNK)
        wc = within[:, c * GATE_LANES:(c + 1) * GATE_LANES]
        glob = wc + carry
        carry = carry + wc[A_CHUNK - 1:A_CHUNK, :]
        gcol_ref[rows, :] = jnp.where(col < A_HEADS, g[rows, :], jnp.where(col < n_a, wc, glob))
    grow_ref[0] = gcol_ref[...].T[0:2 * SUBLANES, :]


def _gates(xb, wg, bias, B, S):
    T, K = xb.shape
    return pl.pallas_call(
        functools.partial(_gates_kernel, S=S),
        grid=(B,),
        in_specs=[
            pl.BlockSpec((S, K), lambda b: (b, 0)),
            pl.BlockSpec((K, GATE_LANES), lambda b: (0, 0)),
            pl.BlockSpec((1, GATE_LANES), lambda b: (0, 0)),
        ],
        out_specs=[
            pl.BlockSpec((S, GATE_LANES), lambda b: (b, 0)),
            pl.BlockSpec((1, 2 * SUBLANES, S), lambda b: (b, 0, 0)),
        ],
        out_shape=[
            jax.ShapeDtypeStruct((T, GATE_LANES), F32),
            jax.ShapeDtypeStruct((B, 2 * SUBLANES, S), F32),
        ],
        compiler_params=_params(("parallel",)),
        name="gates",
    )(xb, wg, bias)


def _mlstm_kernel(q_ref, k_ref, vt_ref, o_ref, gcol_ref, grow_ref, na_ref, out_ref, st_ref, m_ref):
    L = A_CHUNK
    dh = A_HEAD_DIM

    @pl.when(pl.program_id(1) == 0)
    def _():
        st_ref[...] = jnp.zeros_like(st_ref)
        m_ref[...] = jnp.zeros_like(m_ref)

    si = lax.broadcasted_iota(jnp.int32, (L, L), 0)
    ti = lax.broadcasted_iota(jnp.int32, (L, L), 1)
    causal = si <= ti
    ones_row = jnp.where(lax.broadcasted_iota(jnp.int32, (SUBLANES, L), 0) == 0, 1.0, 0.0)
    gcol = gcol_ref[...]
    grow = grow_ref[0]
    for h in range(A_HEADS):
        cs = slice(h * dh, (h + 1) * dh)
        q = q_ref[:, cs]
        k = k_ref[:, cs]
        vt = vt_ref[0, cs, :]
        i_col = gcol[:, h:h + 1]
        b_col = gcol[:, A_HEADS + h:A_HEADS + h + 1]
        i_row = grow[h:h + 1, :]
        b_row = grow[A_HEADS + h:A_HEADS + h + 1, :]
        g = b_row[:, L - 1:L]
        m = m_ref[h]
        dmat = jnp.where(causal, b_row - b_col + i_col, -jnp.inf)
        inter = b_row + m
        m_row = jnp.maximum(inter, jnp.max(dmat, axis=0, keepdims=True))
        w_intra = jnp.exp(dmat - m_row)
        w_inter = jnp.exp(inter - m_row)
        s = _dot_nt(k, q) * w_intra
        state = st_ref[h]
        read = _dot_nt(state.astype(BF16), q)
        num = w_inter * read[0:dh, :] + _dot(vt, s.astype(BF16))
        den = w_inter * read[dh:dh + 1, :] + jnp.sum(s, axis=0, keepdims=True)
        hh = num / jnp.maximum(jnp.abs(den), jnp.exp(-m_row))
        a_row = g - b_row + i_row
        m_new = jnp.maximum(g + m, jnp.max(a_row, axis=1, keepdims=True))
        decay = jnp.exp(g + m - m_new)
        w_row = jnp.exp(a_row - m_new)
        lhs = jnp.concatenate([vt.astype(F32) * w_row, ones_row * w_row], axis=0)
        st_ref[h] = decay * state + _dot(lhs.astype(BF16), k)
        m_ref[h] = m_new
        mu = jnp.mean(hh, axis=0, keepdims=True)
        d = hh - mu
        var = jnp.mean(d * d, axis=0, keepdims=True)
        hn = (d * lax.rsqrt(var + LN_EPS)).T * na_ref[:, cs] * o_ref[:, cs].astype(F32)
        out_ref[:, cs] = hn.astype(out_ref.dtype)


def _mlstm(proj, vt, gcol, grow, norm_a, B, S):
    T = proj.shape[0]
    L = A_CHUNK
    nc = S // L
    blk = lambda cb: pl.BlockSpec((L, BW), lambda b, c, cb=cb: (b * nc + c, cb))
    return pl.pallas_call(
        _mlstm_kernel,
        grid=(B, nc),
        in_specs=[
            blk(PB_AQ), blk(PB_AK),
            pl.BlockSpec((1, BW, L), lambda b, c: (b, VT_A, c)),
            blk(PB_AO),
            pl.BlockSpec((L, GATE_LANES), lambda b, c: (b * nc + c, 0)),
            pl.BlockSpec((1, 2 * SUBLANES, L), lambda b, c: (b, 0, c)),
            pl.BlockSpec((1, BW), lambda b, c: (0, 0)),
        ],
        out_specs=pl.BlockSpec((L, BW), lambda b, c: (b * nc + c, 0)),
        out_shape=jax.ShapeDtypeStruct((T, BW), BF16),
        scratch_shapes=[
            pltpu.VMEM((A_HEADS, A_HEAD_DIM + SUBLANES, A_HEAD_DIM), F32),
            pltpu.VMEM((A_HEADS, 1, 1), F32),
        ],
        compiler_params=_params(("parallel", "arbitrary")),
        name="mlstm",
    )(proj, proj, vt, proj, gcol, grow, norm_a.reshape(1, BW))


def _pool_kernel(u_ref, wp_ref, ps_ref, out_ref, *, S):
    t = lax.broadcasted_iota(jnp.int32, (S, 1), 0)
    gd = B_GROUP_DIM
    for gi, w in enumerate(B_WINDOWS):
        cs = slice(gi * gd, (gi + 1) * gd)
        u = u_ref[:, cs].astype(F32)
        s = u
        k = 1
        while k < w:
            s = s + jnp.where(t >= k, pltpu.roll(s, k, axis=0), 0.0)
            k *= 2
        cnt = jnp.minimum(t + 1, w).astype(F32)
        pooled = s / cnt - u
        y = _dot(pooled.astype(BF16), wp_ref[gi]) * ps_ref[:, cs]
        out_ref[:, cs] = y.astype(out_ref.dtype)


def _pool(proj, w_pool, pool_scale, B, S):
    T = proj.shape[0]
    G = len(B_WINDOWS)
    return pl.pallas_call(
        functools.partial(_pool_kernel, S=S),
        grid=(B,),
        in_specs=[
            pl.BlockSpec((S, BW), lambda b: (b, PB_POOL)),
            pl.BlockSpec((G, B_GROUP_DIM, B_GROUP_DIM), lambda b: (0, 0, 0)),
            pl.BlockSpec((1, BW), lambda b: (0, 0)),
        ],
        out_specs=pl.BlockSpec((S, BW), lambda b: (b, 0)),
        out_shape=jax.ShapeDtypeStruct((T, BW), BF16),
        compiler_params=_params(("parallel",)),
        name="pool",
    )(proj, w_pool, pool_scale.reshape(1, BW))


def _vt_kernel(x_ref, wt_ref, o_ref, *, S):
    rm = min(IN_RM, S)

    def body(mi, carry):
        r0 = pl.multiple_of(mi * rm, rm)
        o_ref[0, :, pl.ds(r0, rm)] = _dot_nt(wt_ref[...], x_ref[pl.ds(r0, rm), :]).astype(o_ref.dtype)
        return carry

    lax.fori_loop(0, S // rm, body, 0, unroll=2)


def _vt(xb, wv_t, B, S):
    K = xb.shape[1]
    nb = wv_t.shape[0] // BW
    return pl.pallas_call(
        functools.partial(_vt_kernel, S=S),
        grid=(B, nb),
        in_specs=[
            pl.BlockSpec((S, K), lambda b, j: (b, 0)),
            pl.BlockSpec((BW, K), lambda b, j: (j, 0)),
        ],
        out_specs=pl.BlockSpec((1, BW, S), lambda b, j: (b, j, 0)),
        out_shape=jax.ShapeDtypeStruct((B, nb * BW, S), BF16),
        compiler_params=_params(("parallel", "arbitrary")),
        name="vt",
    )(xb, wv_t)


def _fox_kernel(q_ref, k_ref, vt_ref, gcol_ref, grow_ref, out_ref, m_ref, l_ref, a_ref, acc_ref,
                s_ref, p_ref, *, tq):
    qi = pl.program_id(1)
    dh = C_HEAD_DIM
    scale = dh ** -0.5
    n_a = 2 * A_HEADS
    m_ref[...] = jnp.full(m_ref.shape, -jnp.inf, F32)
    l_ref[...] = jnp.zeros(l_ref.shape, F32)
    acc_ref[...] = jnp.zeros(acc_ref.shape, F32)

    def block(kj, masked):
        k0 = pl.multiple_of(kj * tq, tq)
        for h in range(C_HEADS):
            cs = slice(h * dh, (h + 1) * dh)
            s_ref[h] = _dot_nt(k_ref[pl.ds(k0, tq), cs], q_ref[:, cs])
        for h in range(C_HEADS):
            c_key = gcol_ref[pl.ds(k0, tq), n_a + h:n_a + h + 1]
            c_qry = grow_ref[0, n_a + h:n_a + h + 1, :]
            lg = s_ref[h] * (scale * LOG2E) + (c_qry * LOG2E - c_key * LOG2E)
            if masked:
                si = lax.broadcasted_iota(jnp.int32, (tq, tq), 0)
                ti = lax.broadcasted_iota(jnp.int32, (tq, tq), 1)
                lg = jnp.where(si <= ti, lg, -jnp.inf)
            s_ref[h] = lg
            m = m_ref[h]
            m_new = jnp.maximum(m, jnp.max(lg, axis=0, keepdims=True))
            a_ref[h] = jnp.exp2(m - m_new)
            m_ref[h] = m_new
        for h in range(C_HEADS):
            p = jnp.exp2(s_ref[h] - m_ref[h])
            l_ref[h] = a_ref[h] * l_ref[h] + jnp.sum(p, axis=0, keepdims=True)
            p_ref[h] = p.astype(BF16)
        for h in range(C_HEADS):
            cs = slice(h * dh, (h + 1) * dh)
            vt = vt_ref[0, cs, pl.ds(k0, tq)]
            acc_ref[h] = a_ref[h] * acc_ref[h] + _dot(vt, p_ref[h])

    def body(kj, carry):
        block(kj, False)
        return carry

    lax.fori_loop(0, qi, body, 0)
    block(qi, True)
    for h in range(C_HEADS):
        cs = slice(h * dh, (h + 1) * dh)
        out_ref[:, cs] = (acc_ref[h] / l_ref[h]).T.astype(out_ref.dtype)


def _fox(proj, vt, gcol, grow, B, S):
    T = proj.shape[0]
    tq = min(FOX_TQ, S)
    nq = S // tq
    return pl.pallas_call(
        functools.partial(_fox_kernel, tq=tq),
        grid=(B, nq),
        in_specs=[
            pl.BlockSpec((tq, BW), lambda b, i: (b * nq + i, PB_CQ)),
            pl.BlockSpec((S, BW), lambda b, i: (b, PB_CK)),
            pl.BlockSpec((1, BW, S), lambda b, i: (b, VT_C, 0)),
            pl.BlockSpec((S, GATE_LANES), lambda b, i: (b, 0)),
            pl.BlockSpec((1, 2 * SUBLANES, tq), lambda b, i: (b, 0, i)),
        ],
        out_specs=pl.BlockSpec((tq, BW), lambda b, i: (b * nq + i, 0)),
        out_shape=jax.ShapeDtypeStruct((T, BW), BF16),
        scratch_shapes=[
            pltpu.VMEM((C_HEADS, 1, tq), F32),
            pltpu.VMEM((C_HEADS, 1, tq), F32),
            pltpu.VMEM((C_HEADS, 1, tq), F32),
            pltpu.VMEM((C_HEADS, C_HEAD_DIM, tq), F32),
            pltpu.VMEM((C_HEADS, tq, tq), F32),
            pltpu.VMEM((C_HEADS, tq, tq), BF16),
        ],
        compiler_params=_params(("parallel", "arbitrary")),
        name="fox",
    )(proj, proj, vt, gcol, grow)


def _merge_kernel(ha_ref, hb_ref, hc_ref, g0_ref, g1_ref, g2_ref, wb_ref, wo_ref, x_ref,
                  lg_ref, lb_ref, xe_ref):
    merged = g0_ref[...].astype(F32) * _dot(ha_ref[...], wb_ref[0])
    merged = merged + g1_ref[...].astype(F32) * _dot(hb_ref[...], wb_ref[1])
    merged = merged + g2_ref[...].astype(F32) * _dot(hc_ref[...], wb_ref[2])
    h = _dot(merged.astype(BF16), wo_ref[...])
    d = x_ref.shape[1]
    xe_ref[:, 0:d] = _ln_rows(ALPHA * x_ref[...] + h, lg_ref[...], lb_ref[...])
    xe_ref[:, d:] = jnp.zeros((xe_ref.shape[0], xe_ref.shape[1] - d), F32)


def _merge(ha, hb, hc, proj, w_branch, w_out, x, ln_g, ln_b):
    T, D = x.shape
    tm = MERGE_TM
    hblk = pl.BlockSpec((tm, BW), lambda i: (i, 0))
    gblk = lambda n: pl.BlockSpec((tm, D), lambda i, n=n: (i, PB_GATES * BW // D + n))
    const = lambda shape: pl.BlockSpec(shape, lambda i: (0,) * len(shape), pipeline_mode=pl.Buffered(1))
    row = pl.BlockSpec((tm, D), lambda i: (i, 0))
    return pl.pallas_call(
        _merge_kernel,
        grid=(T // tm,),
        in_specs=[hblk, hblk, hblk, gblk(0), gblk(1), gblk(2),
                  const((N_BRANCH, BW, D)), const((D, D)), row, const((1, D)), const((1, D))],
        out_specs=pl.BlockSpec((tm, D + LANES), lambda i: (i, 0)),
        out_shape=jax.ShapeDtypeStruct((T, D + LANES), F32),
        compiler_params=_params(("parallel",)),
        name="merge",
    )(ha, hb, hc, proj, proj, proj, w_branch, w_out, x, ln_g.reshape(1, D), ln_b.reshape(1, D))


def _router_kernel(x_ref, wr_ref, br_ref, oi_ref, cnt_ref, we_ref):
    tm = x_ref.shape[0]
    ng, ne = N_GROUPS, EXPERTS_PER_GROUP
    def split(a):
        hi = a.astype(BF16)
        return hi, (a - hi.astype(F32)).astype(BF16)

    x_hi, x_lo = split(x_ref[...])
    w_hi, w_lo = split(wr_ref[...])
    lt = _dot_nt(w_hi, x_hi) + (_dot_nt(w_hi, x_lo) + _dot_nt(w_lo, x_hi))
    logit = [lt[ng * j:ng * (j + 1), :] for j in range(ne)]
    mx = functools.reduce(jnp.maximum, logit)
    mx = jnp.max(mx, axis=0, keepdims=True)
    ex = [jnp.exp(a - mx) for a in logit]
    den = jnp.sum(functools.reduce(jnp.add, ex), axis=0, keepdims=True)
    prob = [e / den for e in ex]
    sel = [prob[j] + br_ref[ng * j:ng * (j + 1), :] for j in range(ne)]

    def first_argmax(vals):
        best = functools.reduce(jnp.maximum, vals)
        idx = jnp.full(best.shape, float(ne - 1), F32)
        for j in range(ne - 2, -1, -1):
            idx = jnp.where(vals[j] == best, float(j), idx)
        return best, idx

    def pick(vals, idx):
        out = vals[ne - 1]
        for j in range(ne - 2, -1, -1):
            out = jnp.where(idx == float(j), vals[j], out)
        return out

    m1, i1 = first_argmax(sel)
    rest = [jnp.where(i1 == float(j), -jnp.inf, sel[j]) for j in range(ne)]
    m2, i2 = first_argmax(rest)
    score = m1 + m2
    gi = lax.broadcasted_iota(jnp.int32, (ng, tm), 0).astype(F32)
    gidx = jnp.min(jnp.where(score == jnp.max(score, axis=0, keepdims=True), gi, float(ng)),
                   axis=0, keepdims=True)
    chosen = gi == gidx
    take = lambda a: jnp.sum(jnp.where(chosen, a, 0.0), axis=0, keepdims=True)
    e1, e2 = take(i1), take(i2)
    p1, p2 = take(pick(prob, i1)), take(pick(prob, i2))
    psum = p1 + p2
    p1, p2 = p1 / psum, p2 / psum
    lo, hi = jnp.minimum(e1, e2), jnp.maximum(e1, e2)
    last = float(N_PAIRS - 1)
    pair = jnp.where(lo == 0.0, hi - 1.0, jnp.where(lo == 1.0, 6.0 - hi, last))
    w_lo = jnp.where(e1 < e2, p1, p2)
    w_hi = jnp.where(e1 < e2, p2, p1)
    w_a = jnp.where(pair == last, w_hi, w_lo)
    w_b = jnp.where(pair == last, w_lo, w_hi)
    cls = gidx * N_PAIRS + pair
    @pl.when(pl.program_id(0) == 0)
    def _():
        cnt_ref[...] = jnp.zeros_like(cnt_ref)

    onehot = lax.broadcasted_iota(jnp.int32, (N_CLASSES, tm), 0).astype(F32) == cls
    jr = lax.broadcasted_iota(jnp.int32, (tm, tm), 0)
    tc = lax.broadcasted_iota(jnp.int32, (tm, tm), 1)
    upper = jnp.where(jr <= tc, 1.0, 0.0).astype(BF16)
    cum = _dot(jnp.where(onehot, 1.0, 0.0).astype(BF16), upper)
    before = cnt_ref[:, 0:1]
    rank = jnp.sum(jnp.where(onehot, cum - 1.0 + before, 0.0), axis=0, keepdims=True)
    cnt_ref[...] = cnt_ref[...] + cum[:, tm - 1:tm]
    zero = jnp.zeros((1, tm), F32)
    rows_i = [cls, gidx * ne + lo, gidx * ne + hi, rank] + [zero] * (SUBLANES - 4)
    oi_ref[...] = jnp.concatenate(rows_i, axis=0).astype(jnp.int32)
    extra = jnp.concatenate([w_a, w_b, jnp.zeros((LANES - 2, tm), F32)], axis=0)
    we_ref[...] = extra.T


def _router(xe, wr_t, br_t):
    T = xe.shape[0]
    D = D_MODEL
    tm = min(ROUTER_TM, T)
    return pl.pallas_call(
        _router_kernel,
        grid=(T // tm,),
        in_specs=[
            pl.BlockSpec((tm, D), lambda i: (i, 0)),
            pl.BlockSpec((N_EXPERTS, D), lambda i: (0, 0)),
            pl.BlockSpec((N_EXPERTS, 1), lambda i: (0, 0)),
        ],
        out_specs=[
            pl.BlockSpec((SUBLANES, tm), lambda i: (0, i)),
            pl.BlockSpec((N_CLASSES, LANES), lambda i: (0, 0)),
            pl.BlockSpec((tm, LANES), lambda i: (i, D // LANES)),
        ],
        out_shape=[
            jax.ShapeDtypeStruct((SUBLANES, T), jnp.int32),
            jax.ShapeDtypeStruct((N_CLASSES, LANES), F32),
            jax.ShapeDtypeStruct((T, D + LANES), F32),
        ],
        input_output_aliases={0: 2},
        compiler_params=_params(("arbitrary",)),
        name="router",
    )(xe, wr_t, br_t)


def _moe_kernel(ea_ref, eb_ref, nrows_ref, src_ref, srcn_ref, x_hbm, w13a_ref, w13b_ref,
                w2a_ref, w2b_ref, lg_ref, lb_ref, out_hbm, xbuf, obuf, sem_in, sem_out, *, tm, nt):
    i = pl.program_id(0)
    slot = i % 2
    n_cur = nrows_ref[i]
    n_next = nrows_ref[jnp.minimum(i + 1, nt - 1)]
    n_prev = nrows_ref[jnp.maximum(i - 1, 0)]
    has_next = (i + 1 < nt) & (n_next > 0)
    has_prev = (i > 0) & (n_prev > 0)

    def gather(idx_ref, s):
        def start(r, c):
            tok = idx_ref[0, 0, r]
            pltpu.make_async_copy(x_hbm.at[pl.ds(tok, 1)], xbuf.at[s, pl.ds(r, 1)],
                                  sem_in.at[s]).start()
            return c

        lax.fori_loop(0, tm, start, 0, unroll=MOE_DMA_UNROLL)

    def wait_gather(s):
        pltpu.make_async_copy(x_hbm.at[pl.ds(0, tm)], xbuf.at[s], sem_in.at[s]).wait()

    def wait_scatter(n):
        p = tm
        while p >= 1:
            @pl.when((n & p) != 0)
            def _(p=p):
                pltpu.make_async_copy(obuf.at[pl.ds(0, p)], out_hbm.at[pl.ds(0, p)], sem_out).wait()

            p //= 2

    @pl.when((i == 0) & (n_cur > 0))
    def _():
        gather(src_ref, 0)

    @pl.when(n_cur > 0)
    def _():
        wait_gather(slot)

    @pl.when(has_next)
    def _():
        gather(srcn_ref, 1 - slot)

    @pl.when(n_cur > 0)
    def _():
        d = out_hbm.shape[1]
        x = xbuf[slot, :, 0:d]
        w_a = xbuf[slot, :, d:d + 1]
        w_b = xbuf[slot, :, d + 1:d + 2]
        xb = x.astype(BF16)

        def expert(w13_ref, w2_ref):
            h = _dot(xb, w13_ref[0])
            a = jax.nn.silu(h[:, :D_FF]) * h[:, D_FF:]
            return _dot(a.astype(BF16), w2_ref[0])

        y = w_a * expert(w13a_ref, w2a_ref)
        y = y + w_b * expert(w13b_ref, w2b_ref)
        z = _ln_rows(ALPHA * x + y, lg_ref[...], lb_ref[...])

        @pl.when(has_prev)
        def _():
            wait_scatter(n_prev)

        obuf[...] = z

        def start_out(r, c):
            tok = src_ref[0, 0, r]
            pltpu.make_async_copy(obuf.at[pl.ds(r, 1)], out_hbm.at[pl.ds(tok, 1)], sem_out).start()
            return c

        @pl.when(n_cur == tm)
        def _():
            lax.fori_loop(0, tm, start_out, 0, unroll=MOE_DMA_UNROLL)

        @pl.when(n_cur < tm)
        def _():
            lax.fori_loop(0, n_cur, start_out, 0)

    @pl.when((n_cur == 0) & has_prev)
    def _():
        wait_scatter(n_prev)

    @pl.when((i == nt - 1) & (n_cur > 0))
    def _():
        wait_scatter(n_cur)


def _moe(xe, oi, cnt, w13, w2, ln_g, ln_b):
    T = xe.shape[0]
    D = D_MODEL
    tm = MOE_TM
    npad = T + N_CLASSES * tm
    nt = npad // tm
    cls, rank = oi[0], oi[3]
    counts = cnt[:, 0].astype(jnp.int32)
    ptiles = (counts + tm - 1) // tm
    tile_end = jnp.cumsum(ptiles)
    tile_off = tile_end - ptiles
    in_class = cls[:, None] == jnp.arange(N_CLASSES, dtype=jnp.int32)[None, :]
    pos = jnp.sum(jnp.where(in_class, tile_off[None, :], 0), axis=1) * tm + rank
    src = jnp.zeros((npad,), jnp.int32).at[pos].set(jnp.arange(T, dtype=jnp.int32))
    tile_id = jnp.arange(nt, dtype=jnp.int32)
    n_used = tile_end[-1]
    tile_cls = jnp.searchsorted(tile_end, jnp.minimum(tile_id, n_used - 1), side="right").astype(jnp.int32)
    tile_rows = jnp.clip(counts[tile_cls] - (tile_id - tile_off[tile_cls]) * tm, 0, tm)
    tile_rows = jnp.where(tile_id < n_used, tile_rows, 0).astype(jnp.int32)
    pair = tile_cls % N_PAIRS
    tile_e1 = (tile_cls // N_PAIRS) * EXPERTS_PER_GROUP + jnp.array(PAIR_SLOT_A, jnp.int32)[pair]
    tile_e2 = (tile_cls // N_PAIRS) * EXPERTS_PER_GROUP + jnp.array(PAIR_SLOT_B, jnp.int32)[pair]
    src3 = src.reshape(nt, 1, tm)

    F2 = 2 * D_FF
    grid_spec = pltpu.PrefetchScalarGridSpec(
        num_scalar_prefetch=3,
        grid=(nt,),
        in_specs=[
            pl.BlockSpec((1, 1, tm), lambda i, e1, e2, va: (i, 0, 0), memory_space=pltpu.SMEM),
            pl.BlockSpec((1, 1, tm), lambda i, e1, e2, va: (jnp.minimum(i + 1, nt - 1), 0, 0),
                         memory_space=pltpu.SMEM),
            pl.BlockSpec(memory_space=pl.ANY),
            pl.BlockSpec((1, D, F2), lambda i, e1, e2, va: (e1[i], 0, 0)),
            pl.BlockSpec((1, D, F2), lambda i, e1, e2, va: (e2[i], 0, 0)),
            pl.BlockSpec((1, D_FF, D), lambda i, e1, e2, va: (e1[i], 0, 0)),
            pl.BlockSpec((1, D_FF, D), lambda i, e1, e2, va: (e2[i], 0, 0)),
            pl.BlockSpec((1, D), lambda i, e1, e2, va: (0, 0)),
            pl.BlockSpec((1, D), lambda i, e1, e2, va: (0, 0)),
        ],
        out_specs=pl.BlockSpec(memory_space=pl.ANY),
        scratch_shapes=[
            pltpu.VMEM((2, tm, D + LANES), F32),
            pltpu.VMEM((tm, D), F32),
            pltpu.SemaphoreType.DMA((2,)),
            pltpu.SemaphoreType.DMA(()),
        ],
    )
    return pl.pallas_call(
        functools.partial(_moe_kernel, tm=tm, nt=nt),
        grid_spec=grid_spec,
        out_shape=jax.ShapeDtypeStruct((T, D), F32),
        compiler_params=_params(("arbitrary",)),
        name="moe",
    )(tile_e1, tile_e2, tile_rows, src3, src3, xe, w13, w13, w2, w2,
      ln_g.reshape(1, D), ln_b.reshape(1, D))


def _pack_layer_weights(w_in, conv_a, b_if_a, b_f_c):
    a_end = OFF_A + 4 * BW
    w_main = jnp.concatenate(
        [w_in[:, OFF_A:OFF_A + 2 * BW], w_in[:, OFF_G:OFF_G + N_G],
         w_in[:, OFF_A + 3 * BW:a_end], w_in[:, OFF_B:OFF_B + BW],
         w_in[:, OFF_C:OFF_C + 2 * BW]], axis=1).astype(BF16)
    wv_t = jnp.concatenate(
        [w_in[:, OFF_A + 2 * BW:OFF_A + 3 * BW], w_in[:, OFF_C + 2 * BW:OFF_C + 3 * BW]],
        axis=1).T.astype(BF16)
    n_gate = 2 * A_HEADS + C_HEADS
    wg = jnp.concatenate([w_in[:, a_end:a_end + 2 * A_HEADS], w_in[:, OFF_C + 3 * BW:OFF_C + N_C]], axis=1)
    wg = jnp.pad(wg, ((0, 0), (0, GATE_LANES - n_gate))).astype(BF16)
    bias = jnp.pad(jnp.concatenate([b_if_a, b_f_c]), (0, GATE_LANES - n_gate)).reshape(1, GATE_LANES)
    k_scale = jnp.concatenate([jnp.ones((BW,), F32), jnp.full((BW,), A_HEAD_DIM ** -0.5, F32)])
    cw = jnp.concatenate([conv_a, k_scale[None, :], jnp.zeros((SUBLANES - A_CONV - 1, 2 * BW), F32)], axis=0)
    cw = jnp.pad(cw, ((0, 0), (0, N_MAIN - 2 * BW)))
    return w_main, wv_t, wg, bias, cw


def _forward(x, ln0_g, ln0_b, w_in, conv_a, b_if_a, norm_a, w_pool, pool_scale, b_f_c,
             w_branch, w_out, ln1_g, ln1_b, w_router, b_router, w13, w2, ln2_g, ln2_b):
    B, S, D = x.shape
    T = B * S
    xf, xb = _ln0(x.reshape(T, D), ln0_g, ln0_b)
    perm = (jnp.arange(N_EXPERTS) % N_GROUPS) * EXPERTS_PER_GROUP + jnp.arange(N_EXPERTS) // N_GROUPS
    wr_t = w_router.T[perm]
    br_t = b_router[perm].reshape(N_EXPERTS, 1)
    for l in range(w_in.shape[0]):
        w_main, wv_t, wg, bias, cw = _pack_layer_weights(w_in[l], conv_a[l], b_if_a[l], b_f_c[l])
        proj = _inproj(xb, w_main, cw, S)
        vt = _vt(xb, wv_t, B, S)
        gcol, grow = _gates(xb, wg, bias, B, S)
        ha = _mlstm(proj, vt, gcol, grow, norm_a[l], B, S)
        hb = _pool(proj, w_pool[l].astype(BF16), pool_scale[l], B, S)
        hc = _fox(proj, vt, gcol, grow, B, S)
        xe = _merge(ha, hb, hc, proj, w_branch[l].astype(BF16), w_out[l].astype(BF16), xf,
                    ln1_g[l], ln1_b[l])
        oi, cnt, xe = _router(xe, wr_t, br_t)
        xf = _moe(xe, oi, cnt, w13[l].astype(BF16), w2[l].astype(BF16), ln2_g[l], ln2_b[l])
        xb = xf.astype(BF16)
    return xf.reshape(B, S, D)


def kernel(x, ln0_g, ln0_b, w_in, conv_a, b_if_a, norm_a, w_pool, pool_scale, b_f_c, w_branch,
           w_out, ln1_g, ln1_b, w_router, b_router, w13, w2, ln2_g, ln2_b):
    return _forward(x, ln0_g, ln0_b, w_in, conv_a, b_if_a, norm_a, w_pool, pool_scale, b_f_c,
                    w_branch, w_out, ln1_g, ln1_b, w_router, b_router, w13, w2, ln2_g, ln2_b)
```

```python
import functools

import jax
import jax.numpy as jnp
from jax import lax
from jax.experimental import pallas as pl
from jax.experimental.pallas import tpu as pltpu

F32 = jnp.float32
BF16 = jnp.bfloat16

D_MODEL = 2048
DEPTH = 2
BW = D_MODEL // 2
A_HEADS = 4
A_HEAD_DIM = BW // A_HEADS
A_CONV = 4
A_CHUNK = 128
B_WINDOWS = (2, 4, 8, 16)
B_GROUP_DIM = BW // len(B_WINDOWS)
C_HEADS = 8
C_HEAD_DIM = BW // C_HEADS
N_BRANCH = 3
OFF_A = 0
N_A = 4 * BW + 2 * A_HEADS
OFF_B = OFF_A + N_A
OFF_C = OFF_B + BW
N_C = 3 * BW + C_HEADS
OFF_G = OFF_C + N_C
N_G = N_BRANCH * D_MODEL
N_EXPERTS = 32
N_GROUPS = 8
EXPERTS_PER_GROUP = N_EXPERTS // N_GROUPS
D_FF = D_MODEL * 3 // 8
ALPHA = (2 * DEPTH) ** 0.25
LN_EPS = 1e-5
LOG2E = 1.4426950408889634

N_MAIN = 2 * BW + N_G + BW + BW + 2 * BW
PB_AQ = 0
PB_AK = 1
PB_GATES = 2
PB_AO = 8
PB_POOL = 9
PB_CQ = 10
PB_CK = 11
VT_A = 0
VT_C = 1
GATE_LANES = 128
N_PAIRS = 6
N_CLASSES = N_GROUPS * N_PAIRS
PAIR_SLOT_A = (0, 0, 0, 1, 1, 3)
PAIR_SLOT_B = (1, 2, 3, 3, 2, 2)

LANES = 128
SUBLANES = 8
VMEM_LIMIT_BYTES = 56 * 1024 * 1024

IN_TN = 1024
IN_RM = 256
IN_UNROLL = 4
EPI_CB = 256
MERGE_TM = 256
MOE_TM = 256
MOE_DMA_UNROLL = 8
FOX_TQ = 256
ROUTER_TM = 512
LN0_TM = 512


def _params(sem, vmem=VMEM_LIMIT_BYTES):
    return pltpu.CompilerParams(dimension_semantics=sem, vmem_limit_bytes=vmem)


def _ln_rows(y, g, b):
    mu = jnp.mean(y, axis=-1, keepdims=True)
    d = y - mu
    var = jnp.mean(d * d, axis=-1, keepdims=True)
    return d * lax.rsqrt(var + LN_EPS) * g + b


def _dot(a, b):
    return jnp.dot(a, b, preferred_element_type=F32)


def _dot_nt(a, b):
    return lax.dot_general(a, b, (((1,), (1,)), ((), ())), preferred_element_type=F32)


def _ln0_kernel(x_ref, g_ref, b_ref, of_ref, ob_ref):
    y = _ln_rows(x_ref[...], g_ref[...], b_ref[...])
    of_ref[...] = y
    ob_ref[...] = y.astype(BF16)


def _ln0(x2d, g, b):
    T, D = x2d.shape
    tm = LN0_TM
    row = pl.BlockSpec((tm, D), lambda i: (i, 0))
    vec = pl.BlockSpec((1, D), lambda i: (0, 0))
    return pl.pallas_call(
        _ln0_kernel,
        grid=(T // tm,),
        in_specs=[row, vec, vec],
        out_specs=[row, row],
        out_shape=[jax.ShapeDtypeStruct((T, D), F32), jax.ShapeDtypeStruct((T, D), BF16)],
        compiler_params=_params(("parallel",)),
        name="ln0",
    )(x2d, g.reshape(1, D), b.reshape(1, D))


def _inproj_kernel(x_ref, w_ref, cw_ref, o_ref, acc_ref, *, segments, tm, tn):
    j = pl.program_id(1)
    rm = min(IN_RM, tm)

    def chunk(mi, kind):
        r0 = pl.multiple_of(mi * rm, rm)
        if kind == "none":
            acc = _dot_nt(x_ref[pl.ds(r0, rm), :], w_ref[0])
            o_ref[pl.ds(r0, rm), :] = acc.astype(o_ref.dtype)
        elif kind == "sigmoid":
            acc = _dot_nt(x_ref[pl.ds(r0, rm), :], w_ref[0])
            o_ref[pl.ds(r0, rm), :] = jax.nn.sigmoid(acc).astype(o_ref.dtype)
        else:
            @pl.when(mi == 0)
            def _():
                acc_ref[...] = jnp.zeros_like(acc_ref)

            sub = lax.broadcasted_iota(jnp.int32, (SUBLANES, EPI_CB), 0)
            for cb in range(tn // EPI_CB):
                cs = slice(cb * EPI_CB, (cb + 1) * EPI_CB)
                u = _dot_nt(x_ref[pl.ds(r0, rm), :], w_ref[0, cs, :])
                halo = acc_ref[:, cs]
                y = cw_ref[A_CONV - 1:A_CONV, cs] * u
                for d in range(1, A_CONV):
                    rolled = pltpu.roll(u, d, axis=0)
                    head = jnp.where(sub < d, pltpu.roll(halo, d, axis=0), rolled[0:SUBLANES])
                    shifted = jnp.concatenate([head, rolled[SUBLANES:]], axis=0)
                    y = y + cw_ref[A_CONV - 1 - d:A_CONV - d, cs] * shifted
                y = y * jax.nn.sigmoid(y) * cw_ref[A_CONV:A_CONV + 1, cs]
                o_ref[pl.ds(r0, rm), cs] = y.astype(o_ref.dtype)
                acc_ref[:, cs] = u[rm - SUBLANES:rm]

    for lo, hi, kind in segments:

        @pl.when((j >= lo) & (j < hi))
        def _():
            def body(mi, carry):
                chunk(mi, kind)
                return carry

            lax.fori_loop(0, tm // rm, body, 0, unroll=IN_UNROLL if kind != "conv" else 1)


def _inproj(xb, w_pack, layer, cw, S):
    T, K = xb.shape
    N = N_MAIN
    tm, tn = S, IN_TN
    t = lambda c: c // tn
    segments = (
        (t(0), t(PB_GATES * BW), "conv"),
        (t(PB_GATES * BW), t(PB_POOL * BW), "sigmoid"),
        (t(PB_POOL * BW), t(N), "none"),
    )
    return pl.pallas_call(
        functools.partial(_inproj_kernel, segments=segments, tm=tm, tn=tn),
        grid=(T // tm, N // tn),
        in_specs=[
            pl.BlockSpec((tm, K), lambda i, j: (i, 0)),
            pl.BlockSpec((1, tn, K), lambda i, j: (layer, j, 0)),
            pl.BlockSpec((SUBLANES, tn), lambda i, j: (0, j)),
        ],
        out_specs=pl.BlockSpec((tm, tn), lambda i, j: (i, j)),
        out_shape=jax.ShapeDtypeStruct((T, N), BF16),
        scratch_shapes=[pltpu.VMEM((SUBLANES, tn), F32)],
        compiler_params=_params(("parallel", "arbitrary")),
        name="inproj",
    )(xb, w_pack, cw)


def _gates_kernel(x_ref, wg_ref, bias_ref, gcol_ref, grow_ref, *, S):
    g = _dot_nt(x_ref[...], wg_ref[0].astype(BF16)) + bias_ref[...]
    col = lax.broadcasted_iota(jnp.int32, (1, GATE_LANES), 1)
    n_a = 2 * A_HEADS
    logf = jnp.where((col >= A_HEADS) & (col < n_a + C_HEADS), jax.nn.log_sigmoid(g), 0.0)
    nch = S // A_CHUNK
    cat = jnp.concatenate([logf[c * A_CHUNK:(c + 1) * A_CHUNK, :] for c in range(nch)], axis=1)
    ri = lax.broadcasted_iota(jnp.int32, (A_CHUNK, A_CHUNK), 0)
    ci = lax.broadcasted_iota(jnp.int32, (A_CHUNK, A_CHUNK), 1)
    tri = jnp.where(ri >= ci, 1.0, 0.0).astype(BF16)
    hi = cat.astype(BF16)
    r1 = cat - hi.astype(F32)
    mid = r1.astype(BF16)
    low = (r1 - mid.astype(F32)).astype(BF16)
    within = _dot(tri, hi) + _dot(tri, mid) + _dot(tri, low)
    carry = jnp.zeros((1, GATE_LANES), F32)
    for c in range(nch):
        rows = slice(c * A_CHUNK, (c + 1) * A_CHUNK)
        wc = within[:, c * GATE_LANES:(c + 1) * GATE_LANES]
        glob = wc + carry
        carry = carry + wc[A_CHUNK - 1:A_CHUNK, :]
        gcol_ref[rows, :] = jnp.where(col < A_HEADS, g[rows, :], jnp.where(col < n_a, wc, glob))
    grow_ref[0] = gcol_ref[...].T[0:2 * SUBLANES, :]


def _gates(xb, wg, layer, bias, B, S):
    T, K = xb.shape
    return pl.pallas_call(
        functools.partial(_gates_kernel, S=S),
        grid=(B,),
        in_specs=[
            pl.BlockSpec((S, K), lambda b: (b, 0)),
            pl.BlockSpec((1, GATE_LANES, K), lambda b: (layer, 0, 0)),
            pl.BlockSpec((1, GATE_LANES), lambda b: (0, 0)),
        ],
        out_specs=[
            pl.BlockSpec((S, GATE_LANES), lambda b: (b, 0)),
            pl.BlockSpec((1, 2 * SUBLANES, S), lambda b: (b, 0, 0)),
        ],
        out_shape=[
            jax.ShapeDtypeStruct((T, GATE_LANES), F32),
            jax.ShapeDtypeStruct((B, 2 * SUBLANES, S), F32),
        ],
        compiler_params=_params(("parallel",)),
        name="gates",
    )(xb, wg, bias)


def _mlstm_kernel(q_ref, k_ref, vt_ref, o_ref, gcol_ref, grow_ref, na_ref, out_ref, st_ref, m_ref):
    L = A_CHUNK
    dh = A_HEAD_DIM

    @pl.when(pl.program_id(1) == 0)
    def _():
        st_ref[...] = jnp.zeros_like(st_ref)
        m_ref[...] = jnp.zeros_like(m_ref)

    si = lax.broadcasted_iota(jnp.int32, (L, L), 0)
    ti = lax.broadcasted_iota(jnp.int32, (L, L), 1)
    causal = si <= ti
    ones_row = jnp.where(lax.broadcasted_iota(jnp.int32, (SUBLANES, L), 0) == 0, 1.0, 0.0)
    gcol = gcol_ref[...]
    grow = grow_ref[0]
    for h in range(A_HEADS):
        cs = slice(h * dh, (h + 1) * dh)
        q = q_ref[:, cs]
        k = k_ref[:, cs]
        vt = vt_ref[0, cs, :]
        i_col = gcol[:, h:h + 1]
        b_col = gcol[:, A_HEADS + h:A_HEADS + h + 1]
        i_row = grow[h:h + 1, :]
        b_row = grow[A_HEADS + h:A_HEADS + h + 1, :]
        g = b_row[:, L - 1:L]
        m = m_ref[h]
        dmat = jnp.where(causal, b_row - b_col + i_col, -jnp.inf)
        inter = b_row + m
        m_row = jnp.maximum(inter, jnp.max(dmat, axis=0, keepdims=True))
        w_intra = jnp.exp(dmat - m_row)
        w_inter = jnp.exp(inter - m_row)
        s = _dot_nt(k, q) * w_intra
        state = st_ref[h]
        read = _dot_nt(state.astype(BF16), q)
        num = w_inter * read[0:dh, :] + _dot(vt, s.astype(BF16))
        den = w_inter * read[dh:dh + 1, :] + jnp.sum(s, axis=0, keepdims=True)
        hh = num / jnp.maximum(jnp.abs(den), jnp.exp(-m_row))
        a_row = g - b_row + i_row
        m_new = jnp.maximum(g + m, jnp.max(a_row, axis=1, keepdims=True))
        decay = jnp.exp(g + m - m_new)
        w_row = jnp.exp(a_row - m_new)
        lhs = jnp.concatenate([vt.astype(F32) * w_row, ones_row * w_row], axis=0)
        st_ref[h] = decay * state + _dot(lhs.astype(BF16), k)
        m_ref[h] = m_new
        mu = jnp.mean(hh, axis=0, keepdims=True)
        d = hh - mu
        var = jnp.mean(d * d, axis=0, keepdims=True)
        hn = (d * lax.rsqrt(var + LN_EPS)).T * na_ref[:, cs] * o_ref[:, cs].astype(F32)
        out_ref[:, cs] = hn.astype(out_ref.dtype)


def _mlstm(proj, vt, gcol, grow, norm_a, B, S):
    T = proj.shape[0]
    L = A_CHUNK
    nc = S // L
    blk = lambda cb: pl.BlockSpec((L, BW), lambda b, c, cb=cb: (b * nc + c, cb))
    return pl.pallas_call(
        _mlstm_kernel,
        grid=(B, nc),
        in_specs=[
            blk(PB_AQ), blk(PB_AK),
            pl.BlockSpec((1, BW, L), lambda b, c: (b, VT_A, c)),
            blk(PB_AO),
            pl.BlockSpec((L, GATE_LANES), lambda b, c: (b * nc + c, 0)),
            pl.BlockSpec((1, 2 * SUBLANES, L), lambda b, c: (b, 0, c)),
            pl.BlockSpec((1, BW), lambda b, c: (0, 0)),
        ],
        out_specs=pl.BlockSpec((L, BW), lambda b, c: (b * nc + c, 0)),
        out_shape=jax.ShapeDtypeStruct((T, BW), BF16),
        scratch_shapes=[
            pltpu.VMEM((A_HEADS, A_HEAD_DIM + SUBLANES, A_HEAD_DIM), F32),
            pltpu.VMEM((A_HEADS, 1, 1), F32),
        ],
        compiler_params=_params(("parallel", "arbitrary")),
        name="mlstm",
    )(proj, proj, vt, proj, gcol, grow, norm_a.reshape(1, BW))


def _pool_kernel(u_ref, wp_ref, ps_ref, out_ref, *, S):
    t = lax.broadcasted_iota(jnp.int32, (S, 1), 0)
    gd = B_GROUP_DIM
    for gi, w in enumerate(B_WINDOWS):
        cs = slice(gi * gd, (gi + 1) * gd)
        u = u_ref[:, cs].astype(F32)
        s = u
        k = 1
        while k < w:
            s = s + jnp.where(t >= k, pltpu.roll(s, k, axis=0), 0.0)
            k *= 2
        cnt = jnp.minimum(t + 1, w).astype(F32)
        pooled = s / cnt - u
        y = _dot(pooled.astype(BF16), wp_ref[0, gi]) * ps_ref[:, cs]
        out_ref[:, cs] = y.astype(out_ref.dtype)


def _pool(proj, w_pool, layer, pool_scale, B, S):
    T = proj.shape[0]
    G = len(B_WINDOWS)
    return pl.pallas_call(
        functools.partial(_pool_kernel, S=S),
        grid=(B,),
        in_specs=[
            pl.BlockSpec((S, BW), lambda b: (b, PB_POOL)),
            pl.BlockSpec((1, G, B_GROUP_DIM, B_GROUP_DIM), lambda b: (layer, 0, 0, 0)),
            pl.BlockSpec((1, BW), lambda b: (0, 0)),
        ],
        out_specs=pl.BlockSpec((S, BW), lambda b: (b, 0)),
        out_shape=jax.ShapeDtypeStruct((T, BW), BF16),
        compiler_params=_params(("parallel",)),
        name="pool",
    )(proj, w_pool, pool_scale.reshape(1, BW))


def _vt_kernel(x_ref, wt_ref, o_ref, *, S):
    rm = min(IN_RM, S)

    def body(mi, carry):
        r0 = pl.multiple_of(mi * rm, rm)
        o_ref[0, :, pl.ds(r0, rm)] = _dot_nt(wt_ref[0], x_ref[pl.ds(r0, rm), :]).astype(o_ref.dtype)
        return carry

    lax.fori_loop(0, S // rm, body, 0, unroll=2)


def _vt(xb, w_pack, layer, B, S):
    K = xb.shape[1]
    nb = w_pack.shape[1] // BW - N_MAIN // BW
    return pl.pallas_call(
        functools.partial(_vt_kernel, S=S),
        grid=(B, nb),
        in_specs=[
            pl.BlockSpec((S, K), lambda b, j: (b, 0)),
            pl.BlockSpec((1, BW, K), lambda b, j: (layer, N_MAIN // BW + j, 0)),
        ],
        out_specs=pl.BlockSpec((1, BW, S), lambda b, j: (b, j, 0)),
        out_shape=jax.ShapeDtypeStruct((B, nb * BW, S), BF16),
        compiler_params=_params(("parallel", "arbitrary")),
        name="vt",
    )(xb, w_pack)


def _fox_kernel(q_ref, k_ref, vt_ref, gcol_ref, grow_ref, out_ref, m_ref, l_ref, a_ref, acc_ref,
                s_ref, p_ref, *, tq):
    qi = pl.program_id(1)
    dh = C_HEAD_DIM
    scale = dh ** -0.5
    n_a = 2 * A_HEADS
    m_ref[...] = jnp.full(m_ref.shape, -jnp.inf, F32)
    l_ref[...] = jnp.zeros(l_ref.shape, F32)
    acc_ref[...] = jnp.zeros(acc_ref.shape, F32)

    def block(kj, masked):
        k0 = pl.multiple_of(kj * tq, tq)
        for h in range(C_HEADS):
            cs = slice(h * dh, (h + 1) * dh)
            s_ref[h] = _dot_nt(k_ref[pl.ds(k0, tq), cs], q_ref[:, cs])
        for h in range(C_HEADS):
            c_key = gcol_ref[pl.ds(k0, tq), n_a + h:n_a + h + 1]
            c_qry = grow_ref[0, n_a + h:n_a + h + 1, :]
            lg = s_ref[h] * (scale * LOG2E) + (c_qry * LOG2E - c_key * LOG2E)
            if masked:
                si = lax.broadcasted_iota(jnp.int32, (tq, tq), 0)
                ti = lax.broadcasted_iota(jnp.int32, (tq, tq), 1)
                lg = jnp.where(si <= ti, lg, -jnp.inf)
            s_ref[h] = lg
            m = m_ref[h]
            m_new = jnp.maximum(m, jnp.max(lg, axis=0, keepdims=True))
            a_ref[h] = jnp.exp2(m - m_new)
            m_ref[h] = m_new
        for h in range(C_HEADS):
            p = jnp.exp2(s_ref[h] - m_ref[h])
            l_ref[h] = a_ref[h] * l_ref[h] + jnp.sum(p, axis=0, keepdims=True)
            p_ref[h] = p.astype(BF16)
        for h in range(C_HEADS):
            cs = slice(h * dh, (h + 1) * dh)
            vt = vt_ref[0, cs, pl.ds(k0, tq)]
            acc_ref[h] = a_ref[h] * acc_ref[h] + _dot(vt, p_ref[h])

    def body(kj, carry):
        block(kj, False)
        return carry

    lax.fori_loop(0, qi, body, 0)
    block(qi, True)
    for h in range(C_HEADS):
        cs = slice(h * dh, (h + 1) * dh)
        out_ref[:, cs] = (acc_ref[h] / l_ref[h]).T.astype(out_ref.dtype)


def _fox(proj, vt, gcol, grow, B, S):
    T = proj.shape[0]
    tq = min(FOX_TQ, S)
    nq = S // tq
    return pl.pallas_call(
        functools.partial(_fox_kernel, tq=tq),
        grid=(B, nq),
        in_specs=[
            pl.BlockSpec((tq, BW), lambda b, i: (b * nq + i, PB_CQ)),
            pl.BlockSpec((S, BW), lambda b, i: (b, PB_CK)),
            pl.BlockSpec((1, BW, S), lambda b, i: (b, VT_C, 0)),
            pl.BlockSpec((S, GATE_LANES), lambda b, i: (b, 0)),
            pl.BlockSpec((1, 2 * SUBLANES, tq), lambda b, i: (b, 0, i)),
        ],
        out_specs=pl.BlockSpec((tq, BW), lambda b, i: (b * nq + i, 0)),
        out_shape=jax.ShapeDtypeStruct((T, BW), BF16),
        scratch_shapes=[
            pltpu.VMEM((C_HEADS, 1, tq), F32),
            pltpu.VMEM((C_HEADS, 1, tq), F32),
            pltpu.VMEM((C_HEADS, 1, tq), F32),
            pltpu.VMEM((C_HEADS, C_HEAD_DIM, tq), F32),
            pltpu.VMEM((C_HEADS, tq, tq), F32),
            pltpu.VMEM((C_HEADS, tq, tq), BF16),
        ],
        compiler_params=_params(("parallel", "arbitrary")),
        name="fox",
    )(proj, proj, vt, gcol, grow)


def _merge_kernel(ha_ref, hb_ref, hc_ref, g0_ref, g1_ref, g2_ref, wb_ref, wo_ref, x_ref,
                  lg_ref, lb_ref, xe_ref):
    merged = g0_ref[...].astype(F32) * _dot(ha_ref[...], wb_ref[0, 0])
    merged = merged + g1_ref[...].astype(F32) * _dot(hb_ref[...], wb_ref[0, 1])
    merged = merged + g2_ref[...].astype(F32) * _dot(hc_ref[...], wb_ref[0, 2])
    h = _dot(merged.astype(BF16), wo_ref[0])
    d = x_ref.shape[1]
    xe_ref[:, 0:d] = _ln_rows(ALPHA * x_ref[...] + h, lg_ref[...], lb_ref[...])
    xe_ref[:, d:] = jnp.zeros((xe_ref.shape[0], xe_ref.shape[1] - d), F32)


def _merge(ha, hb, hc, proj, w_branch, w_out, layer, x, ln_g, ln_b):
    T, D = x.shape
    tm = MERGE_TM
    hblk = pl.BlockSpec((tm, BW), lambda i: (i, 0))
    gblk = lambda n: pl.BlockSpec((tm, D), lambda i, n=n: (i, PB_GATES * BW // D + n))
    const = lambda shape: pl.BlockSpec(shape, lambda i: (0,) * len(shape), pipeline_mode=pl.Buffered(1))
    of_layer = lambda shape: pl.BlockSpec((1,) + shape, lambda i: (layer,) + (0,) * len(shape),
                                          pipeline_mode=pl.Buffered(1))
    row = pl.BlockSpec((tm, D), lambda i: (i, 0))
    return pl.pallas_call(
        _merge_kernel,
        grid=(T // tm,),
        in_specs=[hblk, hblk, hblk, gblk(0), gblk(1), gblk(2),
                  of_layer((N_BRANCH, BW, D)), of_layer((D, D)), row, const((1, D)), const((1, D))],
        out_specs=pl.BlockSpec((tm, D + LANES), lambda i: (i, 0)),
        out_shape=jax.ShapeDtypeStruct((T, D + LANES), F32),
        compiler_params=_params(("parallel",)),
        name="merge",
    )(ha, hb, hc, proj, proj, proj, w_branch, w_out, x, ln_g.reshape(1, D), ln_b.reshape(1, D))


def _router_kernel(x_ref, wr_ref, br_ref, oi_ref, cnt_ref, we_ref):
    tm = x_ref.shape[0]
    ng, ne = N_GROUPS, EXPERTS_PER_GROUP
    def split(a):
        hi = a.astype(BF16)
        return hi, (a - hi.astype(F32)).astype(BF16)

    x_hi, x_lo = split(x_ref[...])
    w_hi, w_lo = split(wr_ref[...])
    lt = _dot_nt(w_hi, x_hi) + (_dot_nt(w_hi, x_lo) + _dot_nt(w_lo, x_hi))
    logit = [lt[ng * j:ng * (j + 1), :] for j in range(ne)]
    mx = functools.reduce(jnp.maximum, logit)
    mx = jnp.max(mx, axis=0, keepdims=True)
    ex = [jnp.exp(a - mx) for a in logit]
    den = jnp.sum(functools.reduce(jnp.add, ex), axis=0, keepdims=True)
    prob = [e / den for e in ex]
    sel = [prob[j] + br_ref[ng * j:ng * (j + 1), :] for j in range(ne)]

    def first_argmax(vals):
        best = functools.reduce(jnp.maximum, vals)
        idx = jnp.full(best.shape, float(ne - 1), F32)
        for j in range(ne - 2, -1, -1):
            idx = jnp.where(vals[j] == best, float(j), idx)
        return best, idx

    def pick(vals, idx):
        out = vals[ne - 1]
        for j in range(ne - 2, -1, -1):
            out = jnp.where(idx == float(j), vals[j], out)
        return out

    m1, i1 = first_argmax(sel)
    rest = [jnp.where(i1 == float(j), -jnp.inf, sel[j]) for j in range(ne)]
    m2, i2 = first_argmax(rest)
    score = m1 + m2
    gi = lax.broadcasted_iota(jnp.int32, (ng, tm), 0).astype(F32)
    gidx = jnp.min(jnp.where(score == jnp.max(score, axis=0, keepdims=True), gi, float(ng)),
                   axis=0, keepdims=True)
    chosen = gi == gidx
    take = lambda a: jnp.sum(jnp.where(chosen, a, 0.0), axis=0, keepdims=True)
    e1, e2 = take(i1), take(i2)
    p1, p2 = take(pick(prob, i1)), take(pick(prob, i2))
    psum = p1 + p2
    p1, p2 = p1 / psum, p2 / psum
    lo, hi = jnp.minimum(e1, e2), jnp.maximum(e1, e2)
    last = float(N_PAIRS - 1)
    pair = jnp.where(lo == 0.0, hi - 1.0, jnp.where(lo == 1.0, 6.0 - hi, last))
    w_lo = jnp.where(e1 < e2, p1, p2)
    w_hi = jnp.where(e1 < e2, p2, p1)
    w_a = jnp.where(pair == last, w_hi, w_lo)
    w_b = jnp.where(pair == last, w_lo, w_hi)
    cls = gidx * N_PAIRS + pair
    @pl.when(pl.program_id(0) == 0)
    def _():
        cnt_ref[...] = jnp.zeros_like(cnt_ref)

    onehot = lax.broadcasted_iota(jnp.int32, (N_CLASSES, tm), 0).astype(F32) == cls
    jr = lax.broadcasted_iota(jnp.int32, (tm, tm), 0)
    tc = lax.broadcasted_iota(jnp.int32, (tm, tm), 1)
    upper = jnp.where(jr <= tc, 1.0, 0.0).astype(BF16)
    cum = _dot(jnp.where(onehot, 1.0, 0.0).astype(BF16), upper)
    before = cnt_ref[:, 0:1]
    rank = jnp.sum(jnp.where(onehot, cum - 1.0 + before, 0.0), axis=0, keepdims=True)
    cnt_ref[...] = cnt_ref[...] + cum[:, tm - 1:tm]
    zero = jnp.zeros((1, tm), F32)
    rows_i = [cls, gidx * ne + lo, gidx * ne + hi, rank] + [zero] * (SUBLANES - 4)
    oi_ref[...] = jnp.concatenate(rows_i, axis=0).astype(jnp.int32)
    extra = jnp.concatenate([w_a, w_b, jnp.zeros((LANES - 2, tm), F32)], axis=0)
    we_ref[...] = extra.T


def _router(xe, wr_t, br_t):
    T = xe.shape[0]
    D = D_MODEL
    tm = min(ROUTER_TM, T)
    return pl.pallas_call(
        _router_kernel,
        grid=(T // tm,),
        in_specs=[
            pl.BlockSpec((tm, D), lambda i: (i, 0)),
            pl.BlockSpec((N_EXPERTS, D), lambda i: (0, 0)),
            pl.BlockSpec((N_EXPERTS, 1), lambda i: (0, 0)),
        ],
        out_specs=[
            pl.BlockSpec((SUBLANES, tm), lambda i: (0, i)),
            pl.BlockSpec((N_CLASSES, LANES), lambda i: (0, 0)),
            pl.BlockSpec((tm, LANES), lambda i: (i, D // LANES)),
        ],
        out_shape=[
            jax.ShapeDtypeStruct((SUBLANES, T), jnp.int32),
            jax.ShapeDtypeStruct((N_CLASSES, LANES), F32),
            jax.ShapeDtypeStruct((T, D + LANES), F32),
        ],
        input_output_aliases={0: 2},
        compiler_params=_params(("arbitrary",)),
        name="router",
    )(xe, wr_t, br_t)


def _moe_kernel(ea_ref, eb_ref, nrows_ref, src_ref, srcn_ref, x_hbm, w13a_ref, w13b_ref,
                w2a_ref, w2b_ref, lg_ref, lb_ref, out_hbm, xbuf, obuf, sem_in, sem_out, *, tm, nt):
    i = pl.program_id(0)
    slot = i % 2
    n_cur = nrows_ref[i]
    n_next = nrows_ref[jnp.minimum(i + 1, nt - 1)]
    n_prev = nrows_ref[jnp.maximum(i - 1, 0)]
    has_next = (i + 1 < nt) & (n_next > 0)
    has_prev = (i > 0) & (n_prev > 0)

    def gather(idx_ref, s):
        def start(r, c):
            tok = idx_ref[0, 0, r]
            pltpu.make_async_copy(x_hbm.at[pl.ds(tok, 1)], xbuf.at[s, pl.ds(r, 1)],
                                  sem_in.at[s]).start()
            return c

        lax.fori_loop(0, tm, start, 0, unroll=MOE_DMA_UNROLL)

    def wait_gather(s):
        pltpu.make_async_copy(x_hbm.at[pl.ds(0, tm)], xbuf.at[s], sem_in.at[s]).wait()

    def wait_scatter(n):
        p = tm
        while p >= 1:
            @pl.when((n & p) != 0)
            def _(p=p):
                pltpu.make_async_copy(obuf.at[pl.ds(0, p)], out_hbm.at[pl.ds(0, p)], sem_out).wait()

            p //= 2

    @pl.when((i == 0) & (n_cur > 0))
    def _():
        gather(src_ref, 0)

    @pl.when(n_cur > 0)
    def _():
        wait_gather(slot)

    @pl.when(has_next)
    def _():
        gather(srcn_ref, 1 - slot)

    @pl.when(n_cur > 0)
    def _():
        d = out_hbm.shape[1]
        x = xbuf[slot, :, 0:d]
        w_a = xbuf[slot, :, d:d + 1]
        w_b = xbuf[slot, :, d + 1:d + 2]
        xb = x.astype(BF16)

        def expert(w13_ref, w2_ref):
            h = _dot(xb, w13_ref[0, 0])
            a = jax.nn.silu(h[:, :D_FF]) * h[:, D_FF:]
            return _dot(a.astype(BF16), w2_ref[0, 0])

        y = w_a * expert(w13a_ref, w2a_ref)
        y = y + w_b * expert(w13b_ref, w2b_ref)
        z = _ln_rows(ALPHA * x + y, lg_ref[...], lb_ref[...])

        @pl.when(has_prev)
        def _():
            wait_scatter(n_prev)

        obuf[...] = z

        def start_out(r, c):
            tok = src_ref[0, 0, r]
            pltpu.make_async_copy(obuf.at[pl.ds(r, 1)], out_hbm.at[pl.ds(tok, 1)], sem_out).start()
            return c

        @pl.when(n_cur == tm)
        def _():
            lax.fori_loop(0, tm, start_out, 0, unroll=MOE_DMA_UNROLL)

        @pl.when(n_cur < tm)
        def _():
            lax.fori_loop(0, n_cur, start_out, 0)

    @pl.when((n_cur == 0) & has_prev)
    def _():
        wait_scatter(n_prev)

    @pl.when((i == nt - 1) & (n_cur > 0))
    def _():
        wait_scatter(n_cur)


def _moe(xe, oi, cnt, w13, w2, layer, ln_g, ln_b):
    T = xe.shape[0]
    D = D_MODEL
    tm = MOE_TM
    npad = T + N_CLASSES * tm
    nt = npad // tm
    cls, rank = oi[0], oi[3]
    counts = cnt[:, 0].astype(jnp.int32)
    ptiles = (counts + tm - 1) // tm
    tile_end = jnp.cumsum(ptiles)
    tile_off = tile_end - ptiles
    in_class = cls[:, None] == jnp.arange(N_CLASSES, dtype=jnp.int32)[None, :]
    pos = jnp.sum(jnp.where(in_class, tile_off[None, :], 0), axis=1) * tm + rank
    src = jnp.zeros((npad,), jnp.int32).at[pos].set(jnp.arange(T, dtype=jnp.int32))
    tile_id = jnp.arange(nt, dtype=jnp.int32)
    n_used = tile_end[-1]
    tile_cls = jnp.searchsorted(tile_end, jnp.minimum(tile_id, n_used - 1), side="right").astype(jnp.int32)
    tile_rows = jnp.clip(counts[tile_cls] - (tile_id - tile_off[tile_cls]) * tm, 0, tm)
    tile_rows = jnp.where(tile_id < n_used, tile_rows, 0).astype(jnp.int32)
    pair = tile_cls % N_PAIRS
    tile_e1 = (tile_cls // N_PAIRS) * EXPERTS_PER_GROUP + jnp.array(PAIR_SLOT_A, jnp.int32)[pair]
    tile_e2 = (tile_cls // N_PAIRS) * EXPERTS_PER_GROUP + jnp.array(PAIR_SLOT_B, jnp.int32)[pair]
    src3 = src.reshape(nt, 1, tm)

    F2 = 2 * D_FF
    grid_spec = pltpu.PrefetchScalarGridSpec(
        num_scalar_prefetch=3,
        grid=(nt,),
        in_specs=[
            pl.BlockSpec((1, 1, tm), lambda i, e1, e2, va: (i, 0, 0), memory_space=pltpu.SMEM),
            pl.BlockSpec((1, 1, tm), lambda i, e1, e2, va: (jnp.minimum(i + 1, nt - 1), 0, 0),
                         memory_space=pltpu.SMEM),
            pl.BlockSpec(memory_space=pl.ANY),
            pl.BlockSpec((1, 1, D, F2), lambda i, e1, e2, va: (layer, e1[i], 0, 0)),
            pl.BlockSpec((1, 1, D, F2), lambda i, e1, e2, va: (layer, e2[i], 0, 0)),
            pl.BlockSpec((1, 1, D_FF, D), lambda i, e1, e2, va: (layer, e1[i], 0, 0)),
            pl.BlockSpec((1, 1, D_FF, D), lambda i, e1, e2, va: (layer, e2[i], 0, 0)),
            pl.BlockSpec((1, D), lambda i, e1, e2, va: (0, 0)),
            pl.BlockSpec((1, D), lambda i, e1, e2, va: (0, 0)),
        ],
        out_specs=pl.BlockSpec(memory_space=pl.ANY),
        scratch_shapes=[
            pltpu.VMEM((2, tm, D + LANES), F32),
            pltpu.VMEM((tm, D), F32),
            pltpu.SemaphoreType.DMA((2,)),
            pltpu.SemaphoreType.DMA(()),
        ],
    )
    return pl.pallas_call(
        functools.partial(_moe_kernel, tm=tm, nt=nt),
        grid_spec=grid_spec,
        out_shape=jax.ShapeDtypeStruct((T, D), F32),
        compiler_params=_params(("arbitrary",)),
        name="moe",
    )(tile_e1, tile_e2, tile_rows, src3, src3, xe, w13, w13, w2, w2,
      ln_g.reshape(1, D), ln_b.reshape(1, D))


def _pack_kernel(off_ref, w_ref, o_ref):
    o_ref[...] = w_ref[...].astype(o_ref.dtype)


def _pack_weights(w_in):
    L, K, _ = w_in.shape
    w_t = jnp.swapaxes(w_in, 1, 2)
    segs = ((OFF_A, 2 * BW), (OFF_G, N_G), (OFF_A + 3 * BW, BW), (OFF_B, BW), (OFF_C, 2 * BW),
            (OFF_A + 2 * BW, BW), (OFF_C + 2 * BW, BW))
    offs = [off + k * BW for off, width in segs for k in range(width // BW)]
    nblk = len(offs)
    grid_spec = pltpu.PrefetchScalarGridSpec(
        num_scalar_prefetch=1,
        grid=(L, nblk),
        in_specs=[pl.BlockSpec((pl.Element(1), pl.Element(BW), pl.Element(K)),
                               lambda l, j, off: (l, pl.multiple_of(off[j], SUBLANES), 0))],
        out_specs=pl.BlockSpec((1, BW, K), lambda l, j, off: (l, j, 0)),
    )
    return pl.pallas_call(
        _pack_kernel,
        grid_spec=grid_spec,
        out_shape=jax.ShapeDtypeStruct((L, nblk * BW, K), BF16),
        compiler_params=_params(("parallel", "parallel")),
        name="pack",
    )(jnp.asarray(offs, jnp.int32), w_t)


def _gate_bias_and_conv(conv_a, b_if_a, b_f_c):
    n_gate = 2 * A_HEADS + C_HEADS
    bias = jnp.pad(jnp.concatenate([b_if_a, b_f_c]), (0, GATE_LANES - n_gate)).reshape(1, GATE_LANES)
    k_scale = jnp.concatenate([jnp.ones((BW,), F32), jnp.full((BW,), A_HEAD_DIM ** -0.5, F32)])
    cw = jnp.concatenate([conv_a, k_scale[None, :], jnp.zeros((SUBLANES - A_CONV - 1, 2 * BW), F32)], axis=0)
    cw = jnp.pad(cw, ((0, 0), (0, N_MAIN - 2 * BW)))
    return bias, cw


def _forward(x, ln0_g, ln0_b, w_in, conv_a, b_if_a, norm_a, w_pool, pool_scale, b_f_c,
             w_branch, w_out, ln1_g, ln1_b, w_router, b_router, w13, w2, ln2_g, ln2_b):
    B, S, D = x.shape
    T = B * S
    xf, xb = _ln0(x.reshape(T, D), ln0_g, ln0_b)
    perm = (jnp.arange(N_EXPERTS) % N_GROUPS) * EXPERTS_PER_GROUP + jnp.arange(N_EXPERTS) // N_GROUPS
    wr_t = w_router.T[perm]
    br_t = b_router[perm].reshape(N_EXPERTS, 1)
    w_pool_b, w_branch_b, w_out_b = w_pool.astype(BF16), w_branch.astype(BF16), w_out.astype(BF16)
    w13_b, w2_b = w13.astype(BF16), w2.astype(BF16)
    w_pack = _pack_weights(w_in)
    a_end = OFF_A + 4 * BW
    n_gate = 2 * A_HEADS + C_HEADS
    wg_t = jnp.concatenate([jnp.swapaxes(w_in[:, :, a_end:a_end + 2 * A_HEADS], 1, 2),
                            jnp.swapaxes(w_in[:, :, OFF_C + 3 * BW:OFF_C + N_C], 1, 2)], axis=1)
    wg_t = jnp.pad(wg_t, ((0, 0), (0, GATE_LANES - n_gate), (0, 0)))
    for l in range(w_in.shape[0]):
        bias, cw = _gate_bias_and_conv(conv_a[l], b_if_a[l], b_f_c[l])
        proj = _inproj(xb, w_pack, l, cw, S)
        vt = _vt(xb, w_pack, l, B, S)
        gcol, grow = _gates(xb, wg_t, l, bias, B, S)
        ha = _mlstm(proj, vt, gcol, grow, norm_a[l], B, S)
        hb = _pool(proj, w_pool_b, l, pool_scale[l], B, S)
        hc = _fox(proj, vt, gcol, grow, B, S)
        xe = _merge(ha, hb, hc, proj, w_branch_b, w_out_b, l, xf, ln1_g[l], ln1_b[l])
        oi, cnt, xe = _router(xe, wr_t, br_t)
        xf = _moe(xe, oi, cnt, w13_b, w2_b, l, ln2_g[l], ln2_b[l])
        xb = xf.astype(BF16)
    return xf.reshape(B, S, D)


def kernel(x, ln0_g, ln0_b, w_in, conv_a, b_if_a, norm_a, w_pool, pool_scale, b_f_c, w_branch,
           w_out, ln1_g, ln1_b, w_router, b_router, w13, w2, ln2_g, ln2_b):
    return _forward(x, ln0_g, ln0_b, w_in, conv_a, b_if_a, norm_a, w_pool, pool_scale, b_f_c,
                    w_branch, w_out, ln1_g, ln1_b, w_router, b_router, w13, w2, ln2_g, ln2_b)
```

```python
import functools

import jax
import jax.numpy as jnp
from jax import lax
from jax.experimental import pallas as pl
from jax.experimental.pallas import tpu as pltpu

F32 = jnp.float32
BF16 = jnp.bfloat16

D_MODEL = 2048
DEPTH = 2
BW = D_MODEL // 2
A_HEADS = 4
A_HEAD_DIM = BW // A_HEADS
A_CONV = 4
A_CHUNK = 128
B_WINDOWS = (2, 4, 8, 16)
B_GROUP_DIM = BW // len(B_WINDOWS)
C_HEADS = 8
C_HEAD_DIM = BW // C_HEADS
N_BRANCH = 3
OFF_A = 0
N_A = 4 * BW + 2 * A_HEADS
OFF_B = OFF_A + N_A
OFF_C = OFF_B + BW
N_C = 3 * BW + C_HEADS
OFF_G = OFF_C + N_C
N_G = N_BRANCH * D_MODEL
N_EXPERTS = 32
N_GROUPS = 8
EXPERTS_PER_GROUP = N_EXPERTS // N_GROUPS
D_FF = D_MODEL * 3 // 8
ALPHA = (2 * DEPTH) ** 0.25
LN_EPS = 1e-5
LOG2E = 1.4426950408889634

N_MAIN = 2 * BW + N_G + BW + BW + 2 * BW
PB_AQ = 0
PB_AK = 1
PB_GATES = 2
PB_AO = 8
PB_POOL = 9
PB_CQ = 10
PB_CK = 11
VT_A = 0
VT_C = 1
GATE_LANES = 128
N_PAIRS = 6
N_CLASSES = N_GROUPS * N_PAIRS
PAIR_SLOT_A = (0, 0, 0, 1, 1, 3)
PAIR_SLOT_B = (1, 2, 3, 3, 2, 2)

LANES = 128
SUBLANES = 8
VMEM_LIMIT_BYTES = 56 * 1024 * 1024

IN_TN = 1024
IN_RM = 256
IN_UNROLL = 4
EPI_CB = 256
MERGE_TM = 256
MOE_TM = 256
MOE_DMA_UNROLL = 8
FOX_TQ = 256
ROUTER_TM = 512
LN0_TM = 512


def _params(sem, vmem=VMEM_LIMIT_BYTES):
    return pltpu.CompilerParams(dimension_semantics=sem, vmem_limit_bytes=vmem)


def _ln_rows(y, g, b):
    mu = jnp.mean(y, axis=-1, keepdims=True)
    d = y - mu
    var = jnp.mean(d * d, axis=-1, keepdims=True)
    return d * lax.rsqrt(var + LN_EPS) * g + b


def _dot(a, b):
    return jnp.dot(a, b, preferred_element_type=F32)


def _dot_nt(a, b):
    return lax.dot_general(a, b, (((1,), (1,)), ((), ())), preferred_element_type=F32)


def _ln0_kernel(x_ref, g_ref, b_ref, ob_ref):
    ob_ref[...] = _ln_rows(x_ref[...], g_ref[...], b_ref[...]).astype(BF16)


def _ln0(x2d, g, b):
    T, D = x2d.shape
    tm = LN0_TM
    row = pl.BlockSpec((tm, D), lambda i: (i, 0))
    vec = pl.BlockSpec((1, D), lambda i: (0, 0))
    return pl.pallas_call(
        _ln0_kernel,
        grid=(T // tm,),
        in_specs=[row, vec, vec],
        out_specs=row,
        out_shape=jax.ShapeDtypeStruct((T, D), BF16),
        compiler_params=_params(("parallel",)),
        name="ln0",
    )(x2d, g.reshape(1, D), b.reshape(1, D))


def _inproj_kernel(x_ref, w_ref, cw_ref, o_ref, acc_ref, *, segments, tm, tn):
    j = pl.program_id(1)
    rm = min(IN_RM, tm)

    def chunk(mi, kind):
        r0 = pl.multiple_of(mi * rm, rm)
        if kind == "none":
            acc = _dot_nt(x_ref[pl.ds(r0, rm), :], w_ref[0])
            o_ref[pl.ds(r0, rm), :] = acc.astype(o_ref.dtype)
        elif kind == "sigmoid":
            acc = _dot_nt(x_ref[pl.ds(r0, rm), :], w_ref[0])
            o_ref[pl.ds(r0, rm), :] = jax.nn.sigmoid(acc).astype(o_ref.dtype)
        else:
            @pl.when(mi == 0)
            def _():
                acc_ref[...] = jnp.zeros_like(acc_ref)

            sub = lax.broadcasted_iota(jnp.int32, (SUBLANES, EPI_CB), 0)
            for cb in range(tn // EPI_CB):
                cs = slice(cb * EPI_CB, (cb + 1) * EPI_CB)
                u = _dot_nt(x_ref[pl.ds(r0, rm), :], w_ref[0, cs, :])
                halo = acc_ref[:, cs]
                y = cw_ref[A_CONV - 1:A_CONV, cs] * u
                for d in range(1, A_CONV):
                    rolled = pltpu.roll(u, d, axis=0)
                    head = jnp.where(sub < d, pltpu.roll(halo, d, axis=0), rolled[0:SUBLANES])
                    shifted = jnp.concatenate([head, rolled[SUBLANES:]], axis=0)
                    y = y + cw_ref[A_CONV - 1 - d:A_CONV - d, cs] * shifted
                y = y * jax.nn.sigmoid(y) * cw_ref[A_CONV:A_CONV + 1, cs]
                o_ref[pl.ds(r0, rm), cs] = y.astype(o_ref.dtype)
                acc_ref[:, cs] = u[rm - SUBLANES:rm]

    for lo, hi, kind in segments:

        @pl.when((j >= lo) & (j < hi))
        def _():
            def body(mi, carry):
                chunk(mi, kind)
                return carry

            lax.fori_loop(0, tm // rm, body, 0, unroll=IN_UNROLL if kind != "conv" else 1)


def _inproj(xb, w_pack, layer, cw, S):
    T, K = xb.shape
    N = N_MAIN
    tm, tn = S, IN_TN
    t = lambda c: c // tn
    segments = (
        (t(0), t(PB_GATES * BW), "conv"),
        (t(PB_GATES * BW), t(PB_POOL * BW), "sigmoid"),
        (t(PB_POOL * BW), t(N), "none"),
    )
    return pl.pallas_call(
        functools.partial(_inproj_kernel, segments=segments, tm=tm, tn=tn),
        grid=(T // tm, N // tn),
        in_specs=[
            pl.BlockSpec((tm, K), lambda i, j: (i, 0)),
            pl.BlockSpec((1, tn, K), lambda i, j: (layer, j, 0)),
            pl.BlockSpec((SUBLANES, tn), lambda i, j: (0, j)),
        ],
        out_specs=pl.BlockSpec((tm, tn), lambda i, j: (i, j)),
        out_shape=jax.ShapeDtypeStruct((T, N), BF16),
        scratch_shapes=[pltpu.VMEM((SUBLANES, tn), F32)],
        compiler_params=_params(("parallel", "arbitrary")),
        name="inproj",
    )(xb, w_pack, cw)


def _gates_kernel(x_ref, wg_ref, bias_ref, gcol_ref, grow_ref, *, S):
    g = _dot_nt(x_ref[...], wg_ref[0].astype(BF16)) + bias_ref[...]
    col = lax.broadcasted_iota(jnp.int32, (1, GATE_LANES), 1)
    n_a = 2 * A_HEADS
    logf = jnp.where((col >= A_HEADS) & (col < n_a + C_HEADS), jax.nn.log_sigmoid(g), 0.0)
    nch = S // A_CHUNK
    cat = jnp.concatenate([logf[c * A_CHUNK:(c + 1) * A_CHUNK, :] for c in range(nch)], axis=1)
    ri = lax.broadcasted_iota(jnp.int32, (A_CHUNK, A_CHUNK), 0)
    ci = lax.broadcasted_iota(jnp.int32, (A_CHUNK, A_CHUNK), 1)
    tri = jnp.where(ri >= ci, 1.0, 0.0).astype(BF16)
    hi = cat.astype(BF16)
    r1 = cat - hi.astype(F32)
    mid = r1.astype(BF16)
    low = (r1 - mid.astype(F32)).astype(BF16)
    within = _dot(tri, hi) + _dot(tri, mid) + _dot(tri, low)
    carry = jnp.zeros((1, GATE_LANES), F32)
    for c in range(nch):
        rows = slice(c * A_CHUNK, (c + 1) * A_CHUNK)
        wc = within[:, c * GATE_LANES:(c + 1) * GATE_LANES]
        glob = wc + carry
        carry = carry + wc[A_CHUNK - 1:A_CHUNK, :]
        gcol_ref[rows, :] = jnp.where(col < A_HEADS, g[rows, :], jnp.where(col < n_a, wc, glob))
    grow_ref[0] = gcol_ref[...].T[0:2 * SUBLANES, :]


def _gates(xb, wg, layer, bias, B, S):
    T, K = xb.shape
    return pl.pallas_call(
        functools.partial(_gates_kernel, S=S),
        grid=(B,),
        in_specs=[
            pl.BlockSpec((S, K), lambda b: (b, 0)),
            pl.BlockSpec((1, GATE_LANES, K), lambda b: (layer, 0, 0)),
            pl.BlockSpec((1, GATE_LANES), lambda b: (0, 0)),
        ],
        out_specs=[
            pl.BlockSpec((S, GATE_LANES), lambda b: (b, 0)),
            pl.BlockSpec((1, 2 * SUBLANES, S), lambda b: (b, 0, 0)),
        ],
        out_shape=[
            jax.ShapeDtypeStruct((T, GATE_LANES), F32),
            jax.ShapeDtypeStruct((B, 2 * SUBLANES, S), F32),
        ],
        compiler_params=_params(("parallel",)),
        name="gates",
    )(xb, wg, bias)


def _mlstm_kernel(q_ref, k_ref, vt_ref, o_ref, gcol_ref, grow_ref, na_ref, out_ref, st_ref, m_ref):
    L = A_CHUNK
    dh = A_HEAD_DIM

    @pl.when(pl.program_id(1) == 0)
    def _():
        st_ref[...] = jnp.zeros_like(st_ref)
        m_ref[...] = jnp.zeros_like(m_ref)

    si = lax.broadcasted_iota(jnp.int32, (L, L), 0)
    ti = lax.broadcasted_iota(jnp.int32, (L, L), 1)
    causal = si <= ti
    ones_row = jnp.where(lax.broadcasted_iota(jnp.int32, (SUBLANES, L), 0) == 0, 1.0, 0.0)
    gcol = gcol_ref[...]
    grow = grow_ref[0]
    for h in range(A_HEADS):
        cs = slice(h * dh, (h + 1) * dh)
        q = q_ref[:, cs]
        k = k_ref[:, cs]
        vt = vt_ref[0, cs, :]
        i_col = gcol[:, h:h + 1]
        b_col = gcol[:, A_HEADS + h:A_HEADS + h + 1]
        i_row = grow[h:h + 1, :]
        b_row = grow[A_HEADS + h:A_HEADS + h + 1, :]
        g = b_row[:, L - 1:L]
        m = m_ref[h]
        dmat = jnp.where(causal, b_row - b_col + i_col, -jnp.inf)
        inter = b_row + m
        m_row = jnp.maximum(inter, jnp.max(dmat, axis=0, keepdims=True))
        w_intra = jnp.exp(dmat - m_row)
        w_inter = jnp.exp(inter - m_row)
        s = _dot_nt(k, q) * w_intra
        state = st_ref[h]
        read = _dot_nt(state.astype(BF16), q)
        num = w_inter * read[0:dh, :] + _dot(vt, s.astype(BF16))
        den = w_inter * read[dh:dh + 1, :] + jnp.sum(s, axis=0, keepdims=True)
        hh = num / jnp.maximum(jnp.abs(den), jnp.exp(-m_row))
        a_row = g - b_row + i_row
        m_new = jnp.maximum(g + m, jnp.max(a_row, axis=1, keepdims=True))
        decay = jnp.exp(g + m - m_new)
        w_row = jnp.exp(a_row - m_new)
        lhs = jnp.concatenate([vt.astype(F32) * w_row, ones_row * w_row], axis=0)
        st_ref[h] = decay * state + _dot(lhs.astype(BF16), k)
        m_ref[h] = m_new
        mu = jnp.mean(hh, axis=0, keepdims=True)
        d = hh - mu
        var = jnp.mean(d * d, axis=0, keepdims=True)
        hn = (d * lax.rsqrt(var + LN_EPS)).T * na_ref[:, cs] * o_ref[:, cs].astype(F32)
        out_ref[:, cs] = hn.astype(out_ref.dtype)


def _mlstm(proj, vt, gcol, grow, norm_a, B, S):
    T = proj.shape[0]
    L = A_CHUNK
    nc = S // L
    blk = lambda cb: pl.BlockSpec((L, BW), lambda b, c, cb=cb: (b * nc + c, cb))
    return pl.pallas_call(
        _mlstm_kernel,
        grid=(B, nc),
        in_specs=[
            blk(PB_AQ), blk(PB_AK),
            pl.BlockSpec((1, BW, L), lambda b, c: (b, VT_A, c)),
            blk(PB_AO),
            pl.BlockSpec((L, GATE_LANES), lambda b, c: (b * nc + c, 0)),
            pl.BlockSpec((1, 2 * SUBLANES, L), lambda b, c: (b, 0, c)),
            pl.BlockSpec((1, BW), lambda b, c: (0, 0)),
        ],
        out_specs=pl.BlockSpec((L, BW), lambda b, c: (b * nc + c, 0)),
        out_shape=jax.ShapeDtypeStruct((T, BW), BF16),
        scratch_shapes=[
            pltpu.VMEM((A_HEADS, A_HEAD_DIM + SUBLANES, A_HEAD_DIM), F32),
            pltpu.VMEM((A_HEADS, 1, 1), F32),
        ],
        compiler_params=_params(("parallel", "arbitrary")),
        name="mlstm",
    )(proj, proj, vt, proj, gcol, grow, norm_a.reshape(1, BW))


def _pool_kernel(u_ref, wp_ref, ps_ref, out_ref, *, S):
    t = lax.broadcasted_iota(jnp.int32, (S, 1), 0)
    gd = B_GROUP_DIM
    for gi, w in enumerate(B_WINDOWS):
        cs = slice(gi * gd, (gi + 1) * gd)
        u = u_ref[:, cs].astype(F32)
        s = u
        k = 1
        while k < w:
            s = s + jnp.where(t >= k, pltpu.roll(s, k, axis=0), 0.0)
            k *= 2
        cnt = jnp.minimum(t + 1, w).astype(F32)
        pooled = s / cnt - u
        y = _dot(pooled.astype(BF16), wp_ref[0, gi]) * ps_ref[:, cs]
        out_ref[:, cs] = y.astype(out_ref.dtype)


def _pool(proj, w_pool, layer, pool_scale, B, S):
    T = proj.shape[0]
    G = len(B_WINDOWS)
    return pl.pallas_call(
        functools.partial(_pool_kernel, S=S),
        grid=(B,),
        in_specs=[
            pl.BlockSpec((S, BW), lambda b: (b, PB_POOL)),
            pl.BlockSpec((1, G, B_GROUP_DIM, B_GROUP_DIM), lambda b: (layer, 0, 0, 0)),
            pl.BlockSpec((1, BW), lambda b: (0, 0)),
        ],
        out_specs=pl.BlockSpec((S, BW), lambda b: (b, 0)),
        out_shape=jax.ShapeDtypeStruct((T, BW), BF16),
        compiler_params=_params(("parallel",)),
        name="pool",
    )(proj, w_pool, pool_scale.reshape(1, BW))


def _vt_kernel(x_ref, wt_ref, o_ref, *, S):
    rm = min(IN_RM, S)

    def body(mi, carry):
        r0 = pl.multiple_of(mi * rm, rm)
        o_ref[0, :, pl.ds(r0, rm)] = _dot_nt(wt_ref[0], x_ref[pl.ds(r0, rm), :]).astype(o_ref.dtype)
        return carry

    lax.fori_loop(0, S // rm, body, 0, unroll=2)


def _vt(xb, w_pack, layer, B, S):
    K = xb.shape[1]
    nb = w_pack.shape[1] // BW - N_MAIN // BW
    return pl.pallas_call(
        functools.partial(_vt_kernel, S=S),
        grid=(B, nb),
        in_specs=[
            pl.BlockSpec((S, K), lambda b, j: (b, 0)),
            pl.BlockSpec((1, BW, K), lambda b, j: (layer, N_MAIN // BW + j, 0)),
        ],
        out_specs=pl.BlockSpec((1, BW, S), lambda b, j: (b, j, 0)),
        out_shape=jax.ShapeDtypeStruct((B, nb * BW, S), BF16),
        compiler_params=_params(("parallel", "arbitrary")),
        name="vt",
    )(xb, w_pack)


def _fox_kernel(q_ref, k_ref, vt_ref, gcol_ref, grow_ref, out_ref, m_ref, l_ref, a_ref, acc_ref,
                s_ref, p_ref, *, tq):
    qi = pl.program_id(1)
    dh = C_HEAD_DIM
    scale = dh ** -0.5
    n_a = 2 * A_HEADS
    m_ref[...] = jnp.full(m_ref.shape, -jnp.inf, F32)
    l_ref[...] = jnp.zeros(l_ref.shape, F32)
    acc_ref[...] = jnp.zeros(acc_ref.shape, F32)

    def block(kj, masked):
        k0 = pl.multiple_of(kj * tq, tq)
        for h in range(C_HEADS):
            cs = slice(h * dh, (h + 1) * dh)
            s_ref[h] = _dot_nt(k_ref[pl.ds(k0, tq), cs], q_ref[:, cs])
        for h in range(C_HEADS):
            c_key = gcol_ref[pl.ds(k0, tq), n_a + h:n_a + h + 1]
            c_qry = grow_ref[0, n_a + h:n_a + h + 1, :]
            lg = s_ref[h] * (scale * LOG2E) + (c_qry * LOG2E - c_key * LOG2E)
            if masked:
                si = lax.broadcasted_iota(jnp.int32, (tq, tq), 0)
                ti = lax.broadcasted_iota(jnp.int32, (tq, tq), 1)
                lg = jnp.where(si <= ti, lg, -jnp.inf)
            s_ref[h] = lg
            m = m_ref[h]
            m_new = jnp.maximum(m, jnp.max(lg, axis=0, keepdims=True))
            a_ref[h] = jnp.exp2(m - m_new)
            m_ref[h] = m_new
        for h in range(C_HEADS):
            p = jnp.exp2(s_ref[h] - m_ref[h])
            l_ref[h] = a_ref[h] * l_ref[h] + jnp.sum(p, axis=0, keepdims=True)
            p_ref[h] = p.astype(BF16)
        for h in range(C_HEADS):
            cs = slice(h * dh, (h + 1) * dh)
            vt = vt_ref[0, cs, pl.ds(k0, tq)]
            acc_ref[h] = a_ref[h] * acc_ref[h] + _dot(vt, p_ref[h])

    def body(kj, carry):
        block(kj, False)
        return carry

    lax.fori_loop(0, qi, body, 0)
    block(qi, True)
    for h in range(C_HEADS):
        cs = slice(h * dh, (h + 1) * dh)
        out_ref[:, cs] = (acc_ref[h] / l_ref[h]).T.astype(out_ref.dtype)


def _fox(proj, vt, gcol, grow, B, S):
    T = proj.shape[0]
    tq = min(FOX_TQ, S)
    nq = S // tq
    return pl.pallas_call(
        functools.partial(_fox_kernel, tq=tq),
        grid=(B, nq),
        in_specs=[
            pl.BlockSpec((tq, BW), lambda b, i: (b * nq + i, PB_CQ)),
            pl.BlockSpec((S, BW), lambda b, i: (b, PB_CK)),
            pl.BlockSpec((1, BW, S), lambda b, i: (b, VT_C, 0)),
            pl.BlockSpec((S, GATE_LANES), lambda b, i: (b, 0)),
            pl.BlockSpec((1, 2 * SUBLANES, tq), lambda b, i: (b, 0, i)),
        ],
        out_specs=pl.BlockSpec((tq, BW), lambda b, i: (b * nq + i, 0)),
        out_shape=jax.ShapeDtypeStruct((T, BW), BF16),
        scratch_shapes=[
            pltpu.VMEM((C_HEADS, 1, tq), F32),
            pltpu.VMEM((C_HEADS, 1, tq), F32),
            pltpu.VMEM((C_HEADS, 1, tq), F32),
            pltpu.VMEM((C_HEADS, C_HEAD_DIM, tq), F32),
            pltpu.VMEM((C_HEADS, tq, tq), F32),
            pltpu.VMEM((C_HEADS, tq, tq), BF16),
        ],
        compiler_params=_params(("parallel", "arbitrary")),
        name="fox",
    )(proj, proj, vt, gcol, grow)


def _merge_kernel(ha_ref, hb_ref, hc_ref, g0_ref, g1_ref, g2_ref, wb_ref, wo_ref, x_ref,
                  pg_ref, pb_ref, lg_ref, lb_ref, xe_ref, *, pre_norm):
    merged = g0_ref[...].astype(F32) * _dot(ha_ref[...], wb_ref[0, 0])
    merged = merged + g1_ref[...].astype(F32) * _dot(hb_ref[...], wb_ref[0, 1])
    merged = merged + g2_ref[...].astype(F32) * _dot(hc_ref[...], wb_ref[0, 2])
    h = _dot(merged.astype(BF16), wo_ref[0])
    d = x_ref.shape[1]
    x = x_ref[...]
    if pre_norm:
        x = _ln_rows(x, pg_ref[...], pb_ref[...])
    xe_ref[:, 0:d] = _ln_rows(ALPHA * x + h, lg_ref[...], lb_ref[...])
    xe_ref[:, d:] = jnp.zeros((xe_ref.shape[0], xe_ref.shape[1] - d), F32)


def _merge(ha, hb, hc, proj, w_branch, w_out, layer, x, pre_norm, pre_g, pre_b, ln_g, ln_b):
    T, D = x.shape
    tm = MERGE_TM
    hblk = pl.BlockSpec((tm, BW), lambda i: (i, 0))
    gblk = lambda n: pl.BlockSpec((tm, D), lambda i, n=n: (i, PB_GATES * BW // D + n))
    const = lambda shape: pl.BlockSpec(shape, lambda i: (0,) * len(shape), pipeline_mode=pl.Buffered(1))
    of_layer = lambda shape: pl.BlockSpec((1,) + shape, lambda i: (layer,) + (0,) * len(shape),
                                          pipeline_mode=pl.Buffered(1))
    row = pl.BlockSpec((tm, D), lambda i: (i, 0))
    return pl.pallas_call(
        functools.partial(_merge_kernel, pre_norm=pre_norm),
        grid=(T // tm,),
        in_specs=[hblk, hblk, hblk, gblk(0), gblk(1), gblk(2),
                  of_layer((N_BRANCH, BW, D)), of_layer((D, D)), row,
                  const((1, D)), const((1, D)), const((1, D)), const((1, D))],
        out_specs=pl.BlockSpec((tm, D + LANES), lambda i: (i, 0)),
        out_shape=jax.ShapeDtypeStruct((T, D + LANES), F32),
        compiler_params=_params(("parallel",)),
        name="merge",
    )(ha, hb, hc, proj, proj, proj, w_branch, w_out, x, pre_g.reshape(1, D), pre_b.reshape(1, D),
      ln_g.reshape(1, D), ln_b.reshape(1, D))


def _router_kernel(x_ref, wr_ref, br_ref, oi_ref, cnt_ref, we_ref):
    tm = x_ref.shape[0]
    ng, ne = N_GROUPS, EXPERTS_PER_GROUP
    def split(a):
        hi = a.astype(BF16)
        return hi, (a - hi.astype(F32)).astype(BF16)

    x_hi, x_lo = split(x_ref[...])
    w_hi, w_lo = split(wr_ref[...])
    lt = _dot_nt(w_hi, x_hi) + (_dot_nt(w_hi, x_lo) + _dot_nt(w_lo, x_hi))
    logit = [lt[ng * j:ng * (j + 1), :] for j in range(ne)]
    mx = functools.reduce(jnp.maximum, logit)
    mx = jnp.max(mx, axis=0, keepdims=True)
    ex = [jnp.exp(a - mx) for a in logit]
    den = jnp.sum(functools.reduce(jnp.add, ex), axis=0, keepdims=True)
    prob = [e / den for e in ex]
    sel = [prob[j] + br_ref[ng * j:ng * (j + 1), :] for j in range(ne)]

    def first_argmax(vals):
        best = functools.reduce(jnp.maximum, vals)
        idx = jnp.full(best.shape, float(ne - 1), F32)
        for j in range(ne - 2, -1, -1):
            idx = jnp.where(vals[j] == best, float(j), idx)
        return best, idx

    def pick(vals, idx):
        out = vals[ne - 1]
        for j in range(ne - 2, -1, -1):
            out = jnp.where(idx == float(j), vals[j], out)
        return out

    m1, i1 = first_argmax(sel)
    rest = [jnp.where(i1 == float(j), -jnp.inf, sel[j]) for j in range(ne)]
    m2, i2 = first_argmax(rest)
    score = m1 + m2
    gi = lax.broadcasted_iota(jnp.int32, (ng, tm), 0).astype(F32)
    gidx = jnp.min(jnp.where(score == jnp.max(score, axis=0, keepdims=True), gi, float(ng)),
                   axis=0, keepdims=True)
    chosen = gi == gidx
    take = lambda a: jnp.sum(jnp.where(chosen, a, 0.0), axis=0, keepdims=True)
    e1, e2 = take(i1), take(i2)
    p1, p2 = take(pick(prob, i1)), take(pick(prob, i2))
    psum = p1 + p2
    p1, p2 = p1 / psum, p2 / psum
    lo, hi = jnp.minimum(e1, e2), jnp.maximum(e1, e2)
    last = float(N_PAIRS - 1)
    pair = jnp.where(lo == 0.0, hi - 1.0, jnp.where(lo == 1.0, 6.0 - hi, last))
    w_lo = jnp.where(e1 < e2, p1, p2)
    w_hi = jnp.where(e1 < e2, p2, p1)
    w_a = jnp.where(pair == last, w_hi, w_lo)
    w_b = jnp.where(pair == last, w_lo, w_hi)
    cls = gidx * N_PAIRS + pair
    @pl.when(pl.program_id(0) == 0)
    def _():
        cnt_ref[...] = jnp.zeros_like(cnt_ref)

    onehot = lax.broadcasted_iota(jnp.int32, (N_CLASSES, tm), 0).astype(F32) == cls
    jr = lax.broadcasted_iota(jnp.int32, (tm, tm), 0)
    tc = lax.broadcasted_iota(jnp.int32, (tm, tm), 1)
    upper = jnp.where(jr <= tc, 1.0, 0.0).astype(BF16)
    cum = _dot(jnp.where(onehot, 1.0, 0.0).astype(BF16), upper)
    before = cnt_ref[:, 0:1]
    rank = jnp.sum(jnp.where(onehot, cum - 1.0 + before, 0.0), axis=0, keepdims=True)
    cnt_ref[...] = cnt_ref[...] + cum[:, tm - 1:tm]
    zero = jnp.zeros((1, tm), F32)
    rows_i = [cls, gidx * ne + lo, gidx * ne + hi, rank] + [zero] * (SUBLANES - 4)
    oi_ref[...] = jnp.concatenate(rows_i, axis=0).astype(jnp.int32)
    extra = jnp.concatenate([w_a, w_b, jnp.zeros((LANES - 2, tm), F32)], axis=0)
    we_ref[...] = extra.T


def _router(xe, wr_t, br_t):
    T = xe.shape[0]
    D = D_MODEL
    tm = min(ROUTER_TM, T)
    return pl.pallas_call(
        _router_kernel,
        grid=(T // tm,),
        in_specs=[
            pl.BlockSpec((tm, D), lambda i: (i, 0)),
            pl.BlockSpec((N_EXPERTS, D), lambda i: (0, 0)),
            pl.BlockSpec((N_EXPERTS, 1), lambda i: (0, 0)),
        ],
        out_specs=[
            pl.BlockSpec((SUBLANES, tm), lambda i: (0, i)),
            pl.BlockSpec((N_CLASSES, LANES), lambda i: (0, 0)),
            pl.BlockSpec((tm, LANES), lambda i: (i, D // LANES)),
        ],
        out_shape=[
            jax.ShapeDtypeStruct((SUBLANES, T), jnp.int32),
            jax.ShapeDtypeStruct((N_CLASSES, LANES), F32),
            jax.ShapeDtypeStruct((T, D + LANES), F32),
        ],
        input_output_aliases={0: 2},
        compiler_params=_params(("arbitrary",)),
        name="router",
    )(xe, wr_t, br_t)


def _moe_kernel(ea_ref, eb_ref, nrows_ref, start_ref, order_ref, x_hbm, w13a_ref, w13b_ref,
                w2a_ref, w2b_ref, lg_ref, lb_ref, out_hbm, xbuf, obuf, sem_in, sem_out, *, tm, nt):
    i = pl.program_id(0)
    slot = i % 2
    n_cur = nrows_ref[i]
    i_next = jnp.minimum(i + 1, nt - 1)
    n_next = nrows_ref[i_next]
    n_prev = nrows_ref[jnp.maximum(i - 1, 0)]
    has_next = (i + 1 < nt) & (n_next > 0)
    has_prev = (i > 0) & (n_prev > 0)

    def gather(tile, s):
        base = start_ref[tile]

        def start(r, c):
            tok = order_ref[base + r]
            pltpu.make_async_copy(x_hbm.at[pl.ds(tok, 1)], xbuf.at[s, pl.ds(r, 1)],
                                  sem_in.at[s]).start()
            return c

        lax.fori_loop(0, tm, start, 0, unroll=MOE_DMA_UNROLL)

    def wait_gather(s):
        pltpu.make_async_copy(x_hbm.at[pl.ds(0, tm)], xbuf.at[s], sem_in.at[s]).wait()

    def wait_scatter(n):
        p = tm
        while p >= 1:
            @pl.when((n & p) != 0)
            def _(p=p):
                pltpu.make_async_copy(obuf.at[pl.ds(0, p)], out_hbm.at[pl.ds(0, p)], sem_out).wait()

            p //= 2

    @pl.when((i == 0) & (n_cur > 0))
    def _():
        gather(i, 0)

    @pl.when(n_cur > 0)
    def _():
        wait_gather(slot)

    @pl.when(has_next)
    def _():
        gather(i_next, 1 - slot)

    @pl.when(n_cur > 0)
    def _():
        d = out_hbm.shape[1]
        x = xbuf[slot, :, 0:d]
        w_a = xbuf[slot, :, d:d + 1]
        w_b = xbuf[slot, :, d + 1:d + 2]
        xb = x.astype(BF16)

        def expert(w13_ref, w2_ref):
            h = _dot(xb, w13_ref[0, 0])
            a = jax.nn.silu(h[:, :D_FF]) * h[:, D_FF:]
            return _dot(a.astype(BF16), w2_ref[0, 0])

        y = w_a * expert(w13a_ref, w2a_ref)
        y = y + w_b * expert(w13b_ref, w2b_ref)
        z = _ln_rows(ALPHA * x + y, lg_ref[...], lb_ref[...])

        @pl.when(has_prev)
        def _():
            wait_scatter(n_prev)

        obuf[...] = z

        base = start_ref[i]

        def start_out(r, c):
            tok = order_ref[base + r]
            pltpu.make_async_copy(obuf.at[pl.ds(r, 1)], out_hbm.at[pl.ds(tok, 1)], sem_out).start()
            return c

        @pl.when(n_cur == tm)
        def _():
            lax.fori_loop(0, tm, start_out, 0, unroll=MOE_DMA_UNROLL)

        @pl.when(n_cur < tm)
        def _():
            lax.fori_loop(0, n_cur, start_out, 0)

    @pl.when((n_cur == 0) & has_prev)
    def _():
        wait_scatter(n_prev)

    @pl.when((i == nt - 1) & (n_cur > 0))
    def _():
        wait_scatter(n_cur)


def _moe(xe, oi, cnt, w13, w2, layer, ln_g, ln_b):
    T = xe.shape[0]
    D = D_MODEL
    tm = MOE_TM
    npad = T + N_CLASSES * tm
    nt = npad // tm
    cls, rank = oi[0], oi[3]
    counts = cnt[:, 0].astype(jnp.int32)
    ptiles = (counts + tm - 1) // tm
    tile_end = jnp.cumsum(ptiles)
    tile_off = tile_end - ptiles
    row_off = jnp.cumsum(counts) - counts
    order = jnp.argsort(cls * T + rank).astype(jnp.int32)
    order = jnp.concatenate([order, jnp.zeros((tm,), jnp.int32)])
    tile_id = jnp.arange(nt, dtype=jnp.int32)
    n_used = tile_end[-1]
    tile_cls = jnp.searchsorted(tile_end, jnp.minimum(tile_id, n_used - 1), side="right").astype(jnp.int32)
    in_class_tile = (tile_id - tile_off[tile_cls]) * tm
    tile_rows = jnp.clip(counts[tile_cls] - in_class_tile, 0, tm)
    tile_rows = jnp.where(tile_id < n_used, tile_rows, 0).astype(jnp.int32)
    tile_start = jnp.where(tile_id < n_used, row_off[tile_cls] + in_class_tile, 0).astype(jnp.int32)
    pair = tile_cls % N_PAIRS
    tile_e1 = (tile_cls // N_PAIRS) * EXPERTS_PER_GROUP + jnp.array(PAIR_SLOT_A, jnp.int32)[pair]
    tile_e2 = (tile_cls // N_PAIRS) * EXPERTS_PER_GROUP + jnp.array(PAIR_SLOT_B, jnp.int32)[pair]

    F2 = 2 * D_FF
    grid_spec = pltpu.PrefetchScalarGridSpec(
        num_scalar_prefetch=5,
        grid=(nt,),
        in_specs=[
            pl.BlockSpec(memory_space=pl.ANY),
            pl.BlockSpec((1, 1, D, F2), lambda i, e1, e2, *_: (layer, e1[i], 0, 0)),
            pl.BlockSpec((1, 1, D, F2), lambda i, e1, e2, *_: (layer, e2[i], 0, 0)),
            pl.BlockSpec((1, 1, D_FF, D), lambda i, e1, e2, *_: (layer, e1[i], 0, 0)),
            pl.BlockSpec((1, 1, D_FF, D), lambda i, e1, e2, *_: (layer, e2[i], 0, 0)),
            pl.BlockSpec((1, D), lambda i, *_: (0, 0)),
            pl.BlockSpec((1, D), lambda i, *_: (0, 0)),
        ],
        out_specs=pl.BlockSpec(memory_space=pl.ANY),
        scratch_shapes=[
            pltpu.VMEM((2, tm, D + LANES), F32),
            pltpu.VMEM((tm, D), F32),
            pltpu.SemaphoreType.DMA((2,)),
            pltpu.SemaphoreType.DMA(()),
        ],
    )
    return pl.pallas_call(
        functools.partial(_moe_kernel, tm=tm, nt=nt),
        grid_spec=grid_spec,
        out_shape=jax.ShapeDtypeStruct((T, D), F32),
        compiler_params=_params(("arbitrary",)),
        name="moe",
    )(tile_e1, tile_e2, tile_rows, tile_start, order, xe, w13, w13, w2, w2,
      ln_g.reshape(1, D), ln_b.reshape(1, D))


def _pack_kernel(off_ref, w_ref, o_ref):
    o_ref[...] = w_ref[...].astype(o_ref.dtype)


def _pack_weights(w_in):
    L, K, _ = w_in.shape
    w_t = jnp.swapaxes(w_in, 1, 2)
    segs = ((OFF_A, 2 * BW), (OFF_G, N_G), (OFF_A + 3 * BW, BW), (OFF_B, BW), (OFF_C, 2 * BW),
            (OFF_A + 2 * BW, BW), (OFF_C + 2 * BW, BW))
    offs = [off + k * BW for off, width in segs for k in range(width // BW)]
    nblk = len(offs)
    grid_spec = pltpu.PrefetchScalarGridSpec(
        num_scalar_prefetch=1,
        grid=(L, nblk),
        in_specs=[pl.BlockSpec((pl.Element(1), pl.Element(BW), pl.Element(K)),
                               lambda l, j, off: (l, pl.multiple_of(off[j], SUBLANES), 0))],
        out_specs=pl.BlockSpec((1, BW, K), lambda l, j, off: (l, j, 0)),
    )
    return pl.pallas_call(
        _pack_kernel,
        grid_spec=grid_spec,
        out_shape=jax.ShapeDtypeStruct((L, nblk * BW, K), BF16),
        compiler_params=_params(("parallel", "parallel")),
        name="pack",
    )(jnp.asarray(offs, jnp.int32), w_t)


def _gate_bias_and_conv(conv_a, b_if_a, b_f_c):
    n_gate = 2 * A_HEADS + C_HEADS
    bias = jnp.pad(jnp.concatenate([b_if_a, b_f_c]), (0, GATE_LANES - n_gate)).reshape(1, GATE_LANES)
    k_scale = jnp.concatenate([jnp.ones((BW,), F32), jnp.full((BW,), A_HEAD_DIM ** -0.5, F32)])
    cw = jnp.concatenate([conv_a, k_scale[None, :], jnp.zeros((SUBLANES - A_CONV - 1, 2 * BW), F32)], axis=0)
    cw = jnp.pad(cw, ((0, 0), (0, N_MAIN - 2 * BW)))
    return bias, cw


def _forward(x, ln0_g, ln0_b, w_in, conv_a, b_if_a, norm_a, w_pool, pool_scale, b_f_c,
             w_branch, w_out, ln1_g, ln1_b, w_router, b_router, w13, w2, ln2_g, ln2_b):
    B, S, D = x.shape
    T = B * S
    xf = x.reshape(T, D)
    xb = _ln0(xf, ln0_g, ln0_b)
    perm = (jnp.arange(N_EXPERTS) % N_GROUPS) * EXPERTS_PER_GROUP + jnp.arange(N_EXPERTS) // N_GROUPS
    wr_t = w_router.T[perm]
    br_t = b_router[perm].reshape(N_EXPERTS, 1)
    w_pool_b, w_branch_b, w_out_b = w_pool.astype(BF16), w_branch.astype(BF16), w_out.astype(BF16)
    w13_b, w2_b = w13.astype(BF16), w2.astype(BF16)
    w_pack = _pack_weights(w_in)
    a_end = OFF_A + 4 * BW
    n_gate = 2 * A_HEADS + C_HEADS
    wg_t = jnp.concatenate([jnp.swapaxes(w_in[:, :, a_end:a_end + 2 * A_HEADS], 1, 2),
                            jnp.swapaxes(w_in[:, :, OFF_C + 3 * BW:OFF_C + N_C], 1, 2)], axis=1)
    wg_t = jnp.pad(wg_t, ((0, 0), (0, GATE_LANES - n_gate), (0, 0)))
    for l in range(w_in.shape[0]):
        bias, cw = _gate_bias_and_conv(conv_a[l], b_if_a[l], b_f_c[l])
        proj = _inproj(xb, w_pack, l, cw, S)
        vt = _vt(xb, w_pack, l, B, S)
        gcol, grow = _gates(xb, wg_t, l, bias, B, S)
        ha = _mlstm(proj, vt, gcol, grow, norm_a[l], B, S)
        hb = _pool(proj, w_pool_b, l, pool_scale[l], B, S)
        hc = _fox(proj, vt, gcol, grow, B, S)
        xe = _merge(ha, hb, hc, proj, w_branch_b, w_out_b, l, xf, l == 0, ln0_g, ln0_b,
                    ln1_g[l], ln1_b[l])
        oi, cnt, xe = _router(xe, wr_t, br_t)
        xf = _moe(xe, oi, cnt, w13_b, w2_b, l, ln2_g[l], ln2_b[l])
        xb = xf.astype(BF16)
    return xf.reshape(B, S, D)


def kernel(x, ln0_g, ln0_b, w_in, conv_a, b_if_a, norm_a, w_pool, pool_scale, b_f_c, w_branch,
           w_out, ln1_g, ln1_b, w_router, b_router, w13, w2, ln2_g, ln2_b):
    return _forward(x, ln0_g, ln0_b, w_in, conv_a, b_if_a, norm_a, w_pool, pool_scale, b_f_c,
                    w_branch, w_out, ln1_g, ln1_b, w_router, b_router, w13, w2, ln2_g, ln2_b)
```

```python
import functools

import jax
import jax.numpy as jnp
from jax import lax
from jax.experimental import pallas as pl
from jax.experimental.pallas import tpu as pltpu

F32 = jnp.float32
BF16 = jnp.bfloat16

D_MODEL = 2048
DEPTH = 2
BW = D_MODEL // 2
A_HEADS = 4
A_HEAD_DIM = BW // A_HEADS
A_CONV = 4
A_CHUNK = 128
B_WINDOWS = (2, 4, 8, 16)
B_GROUP_DIM = BW // len(B_WINDOWS)
C_HEADS = 8
C_HEAD_DIM = BW // C_HEADS
N_BRANCH = 3
OFF_A = 0
N_A = 4 * BW + 2 * A_HEADS
OFF_B = OFF_A + N_A
OFF_C = OFF_B + BW
N_C = 3 * BW + C_HEADS
OFF_G = OFF_C + N_C
N_G = N_BRANCH * D_MODEL
N_EXPERTS = 32
N_GROUPS = 8
EXPERTS_PER_GROUP = N_EXPERTS // N_GROUPS
D_FF = D_MODEL * 3 // 8
ALPHA = (2 * DEPTH) ** 0.25
LN_EPS = 1e-5
LOG2E = 1.4426950408889634

N_MAIN = 2 * BW + N_G + BW + BW + 2 * BW
PB_AQ = 0
PB_AK = 1
PB_GATES = 2
PB_AO = 8
PB_POOL = 9
PB_CQ = 10
PB_CK = 11
VT_A = 0
VT_C = 1
GATE_LANES = 128
N_PAIRS = 6
N_CLASSES = N_GROUPS * N_PAIRS
PAIR_SLOT_A = (0, 0, 0, 1, 1, 3)
PAIR_SLOT_B = (1, 2, 3, 3, 2, 2)

LANES = 128
SUBLANES = 8
VMEM_LIMIT_BYTES = 56 * 1024 * 1024

IN_TN = 1024
IN_RM = 256
IN_UNROLL = 4
EPI_CB = 256
MERGE_TM = 256
MOE_TM = 256
MOE_DMA_UNROLL = 8
FOX_TQ = 256
ROUTER_TM = 512
LN0_TM = 512


def _params(sem, vmem=VMEM_LIMIT_BYTES):
    return pltpu.CompilerParams(dimension_semantics=sem, vmem_limit_bytes=vmem)


def _ln_rows(y, g, b):
    mu = jnp.mean(y, axis=-1, keepdims=True)
    d = y - mu
    var = jnp.mean(d * d, axis=-1, keepdims=True)
    return d * lax.rsqrt(var + LN_EPS) * g + b


def _dot(a, b):
    return jnp.dot(a, b, preferred_element_type=F32)


def _dot_nt(a, b):
    return lax.dot_general(a, b, (((1,), (1,)), ((), ())), preferred_element_type=F32)


def _ln0_kernel(x_ref, g_ref, b_ref, ob_ref):
    ob_ref[...] = _ln_rows(x_ref[...], g_ref[...], b_ref[...]).astype(BF16)


def _ln0(x2d, g, b):
    T, D = x2d.shape
    tm = LN0_TM
    row = pl.BlockSpec((tm, D), lambda i: (i, 0))
    vec = pl.BlockSpec((1, D), lambda i: (0, 0))
    return pl.pallas_call(
        _ln0_kernel,
        grid=(T // tm,),
        in_specs=[row, vec, vec],
        out_specs=row,
        out_shape=jax.ShapeDtypeStruct((T, D), BF16),
        compiler_params=_params(("parallel",)),
        name="ln0",
    )(x2d, g.reshape(1, D), b.reshape(1, D))


def _inproj_kernel(x_ref, w_ref, cw_ref, o_ref, acc_ref, *, segments, tm, tn):
    j = pl.program_id(1)
    rm = min(IN_RM, tm)

    def chunk(mi, kind):
        r0 = pl.multiple_of(mi * rm, rm)
        if kind == "none":
            acc = _dot_nt(x_ref[pl.ds(r0, rm), :], w_ref[0])
            o_ref[pl.ds(r0, rm), :] = acc.astype(o_ref.dtype)
        elif kind == "sigmoid":
            acc = _dot_nt(x_ref[pl.ds(r0, rm), :], w_ref[0])
            o_ref[pl.ds(r0, rm), :] = jax.nn.sigmoid(acc).astype(o_ref.dtype)
        else:
            @pl.when(mi == 0)
            def _():
                acc_ref[...] = jnp.zeros_like(acc_ref)

            sub = lax.broadcasted_iota(jnp.int32, (SUBLANES, EPI_CB), 0)
            for cb in range(tn // EPI_CB):
                cs = slice(cb * EPI_CB, (cb + 1) * EPI_CB)
                u = _dot_nt(x_ref[pl.ds(r0, rm), :], w_ref[0, cs, :])
                halo = acc_ref[:, cs]
                y = cw_ref[A_CONV - 1:A_CONV, cs] * u
                for d in range(1, A_CONV):
                    rolled = pltpu.roll(u, d, axis=0)
                    head = jnp.where(sub < d, pltpu.roll(halo, d, axis=0), rolled[0:SUBLANES])
                    shifted = jnp.concatenate([head, rolled[SUBLANES:]], axis=0)
                    y = y + cw_ref[A_CONV - 1 - d:A_CONV - d, cs] * shifted
                y = y * jax.nn.sigmoid(y) * cw_ref[A_CONV:A_CONV + 1, cs]
                o_ref[pl.ds(r0, rm), cs] = y.astype(o_ref.dtype)
                acc_ref[:, cs] = u[rm - SUBLANES:rm]

    for lo, hi, kind in segments:

        @pl.when((j >= lo) & (j < hi))
        def _():
            def body(mi, carry):
                chunk(mi, kind)
                return carry

            lax.fori_loop(0, tm // rm, body, 0, unroll=IN_UNROLL if kind != "conv" else 1)


def _inproj(xb, w_pack, layer, cw, S):
    T, K = xb.shape
    N = N_MAIN
    tm, tn = S, IN_TN
    t = lambda c: c // tn
    segments = (
        (t(0), t(PB_GATES * BW), "conv"),
        (t(PB_GATES * BW), t(PB_POOL * BW), "sigmoid"),
        (t(PB_POOL * BW), t(N), "none"),
    )
    return pl.pallas_call(
        functools.partial(_inproj_kernel, segments=segments, tm=tm, tn=tn),
        grid=(T // tm, N // tn),
        in_specs=[
            pl.BlockSpec((tm, K), lambda i, j: (i, 0)),
            pl.BlockSpec((1, tn, K), lambda i, j: (layer, j, 0)),
            pl.BlockSpec((SUBLANES, tn), lambda i, j: (0, j)),
        ],
        out_specs=pl.BlockSpec((tm, tn), lambda i, j: (i, j)),
        out_shape=jax.ShapeDtypeStruct((T, N), BF16),
        scratch_shapes=[pltpu.VMEM((SUBLANES, tn), F32)],
        compiler_params=_params(("parallel", "arbitrary")),
        name="inproj",
    )(xb, w_pack, cw)


def _gates_kernel(x_ref, wg_ref, bias_ref, gcol_ref, grow_ref, *, S):
    g = _dot_nt(x_ref[...], wg_ref[0].astype(BF16)) + bias_ref[...]
    col = lax.broadcasted_iota(jnp.int32, (1, GATE_LANES), 1)
    n_a = 2 * A_HEADS
    logf = jnp.where((col >= A_HEADS) & (col < n_a + C_HEADS), jax.nn.log_sigmoid(g), 0.0)
    nch = S // A_CHUNK
    cat = jnp.concatenate([logf[c * A_CHUNK:(c + 1) * A_CHUNK, :] for c in range(nch)], axis=1)
    ri = lax.broadcasted_iota(jnp.int32, (A_CHUNK, A_CHUNK), 0)
    ci = lax.broadcasted_iota(jnp.int32, (A_CHUNK, A_CHUNK), 1)
    tri = jnp.where(ri >= ci, 1.0, 0.0).astype(BF16)
    hi = cat.astype(BF16)
    r1 = cat - hi.astype(F32)
    mid = r1.astype(BF16)
    low = (r1 - mid.astype(F32)).astype(BF16)
    within = _dot(tri, hi) + _dot(tri, mid) + _dot(tri, low)
    carry = jnp.zeros((1, GATE_LANES), F32)
    for c in range(nch):
        rows = slice(c * A_CHUNK, (c + 1) * A_CHUNK)
        wc = within[:, c * GATE_LANES:(c + 1) * GATE_LANES]
        glob = wc + carry
        carry = carry + wc[A_CHUNK - 1:A_CHUNK, :]
        gcol_ref[rows, :] = jnp.where(col < A_HEADS, g[rows, :], jnp.where(col < n_a, wc, glob))
    grow_ref[0] = gcol_ref[...].T[0:2 * SUBLANES, :]


def _gates(xb, wg, layer, bias, B, S):
    T, K = xb.shape
    return pl.pallas_call(
        functools.partial(_gates_kernel, S=S),
        grid=(B,),
        in_specs=[
            pl.BlockSpec((S, K), lambda b: (b, 0)),
            pl.BlockSpec((1, GATE_LANES, K), lambda b: (layer, 0, 0)),
            pl.BlockSpec((1, GATE_LANES), lambda b: (0, 0)),
        ],
        out_specs=[
            pl.BlockSpec((S, GATE_LANES), lambda b: (b, 0)),
            pl.BlockSpec((1, 2 * SUBLANES, S), lambda b: (b, 0, 0)),
        ],
        out_shape=[
            jax.ShapeDtypeStruct((T, GATE_LANES), F32),
            jax.ShapeDtypeStruct((B, 2 * SUBLANES, S), F32),
        ],
        compiler_params=_params(("parallel",)),
        name="gates",
    )(xb, wg, bias)


def _mlstm_kernel(q_ref, k_ref, vt_ref, o_ref, gcol_ref, grow_ref, na_ref, out_ref, st_ref, m_ref,
                  dec_ref, row_ref, s_ref, hh_ref):
    L = A_CHUNK
    dh = A_HEAD_DIM

    @pl.when(pl.program_id(1) == 0)
    def _():
        st_ref[...] = jnp.zeros_like(st_ref)
        m_ref[...] = jnp.zeros_like(m_ref)

    si = lax.broadcasted_iota(jnp.int32, (L, L), 0)
    ti = lax.broadcasted_iota(jnp.int32, (L, L), 1)
    causal = si <= ti
    ones_row = jnp.where(lax.broadcasted_iota(jnp.int32, (SUBLANES, L), 0) == 0, 1.0, 0.0)
    gcol = gcol_ref[...]
    grow = grow_ref[0]
    for h in range(A_HEADS):
        cs = slice(h * dh, (h + 1) * dh)
        i_col = gcol[:, h:h + 1]
        b_col = gcol[:, A_HEADS + h:A_HEADS + h + 1]
        i_row = grow[h:h + 1, :]
        b_row = grow[A_HEADS + h:A_HEADS + h + 1, :]
        g = b_row[:, L - 1:L]
        m = m_ref[h]
        dmat = jnp.where(causal, b_row - b_col + i_col, -jnp.inf)
        inter = b_row + m
        m_row = jnp.maximum(inter, jnp.max(dmat, axis=0, keepdims=True))
        w_intra = jnp.exp(dmat - m_row)
        s_ref[h] = _dot_nt(k_ref[:, cs], q_ref[:, cs]) * w_intra
        a_row = g - b_row + i_row
        m_new = jnp.maximum(g + m, jnp.max(a_row, axis=1, keepdims=True))
        row_ref[h, 0:1, :] = jnp.exp(inter - m_row)
        row_ref[h, 1:2, :] = jnp.exp(-m_row)
        row_ref[h, 2:3, :] = jnp.exp(a_row - m_new)
        dec_ref[h] = jnp.exp(g + m - m_new)
        m_ref[h] = m_new
    for h in range(A_HEADS):
        cs = slice(h * dh, (h + 1) * dh)
        q = q_ref[:, cs]
        s = s_ref[h]
        w_inter = row_ref[h, 0:1, :]
        read = _dot_nt(st_ref[h].astype(BF16), q)
        num = w_inter * read[0:dh, :] + _dot(vt_ref[0, cs, :], s.astype(BF16))
        den = w_inter * read[dh:dh + 1, :] + jnp.sum(s, axis=0, keepdims=True)
        hh_ref[h] = num / jnp.maximum(jnp.abs(den), row_ref[h, 1:2, :])
    for h in range(A_HEADS):
        cs = slice(h * dh, (h + 1) * dh)
        w_row = row_ref[h, 2:3, :]
        lhs = jnp.concatenate([vt_ref[0, cs, :].astype(F32) * w_row, ones_row * w_row], axis=0)
        st_ref[h] = dec_ref[h] * st_ref[h] + _dot(lhs.astype(BF16), k_ref[:, cs])
    for h in range(A_HEADS):
        cs = slice(h * dh, (h + 1) * dh)
        hh = hh_ref[h]
        mu = jnp.mean(hh, axis=0, keepdims=True)
        d = hh - mu
        var = jnp.mean(d * d, axis=0, keepdims=True)
        hn = (d * lax.rsqrt(var + LN_EPS)).T * na_ref[:, cs] * o_ref[:, cs].astype(F32)
        out_ref[:, cs] = hn.astype(out_ref.dtype)


def _mlstm(proj, vt, gcol, grow, norm_a, B, S):
    T = proj.shape[0]
    L = A_CHUNK
    nc = S // L
    blk = lambda cb: pl.BlockSpec((L, BW), lambda b, c, cb=cb: (b * nc + c, cb))
    return pl.pallas_call(
        _mlstm_kernel,
        grid=(B, nc),
        in_specs=[
            blk(PB_AQ), blk(PB_AK),
            pl.BlockSpec((1, BW, L), lambda b, c: (b, VT_A, c)),
            blk(PB_AO),
            pl.BlockSpec((L, GATE_LANES), lambda b, c: (b * nc + c, 0)),
            pl.BlockSpec((1, 2 * SUBLANES, L), lambda b, c: (b, 0, c)),
            pl.BlockSpec((1, BW), lambda b, c: (0, 0)),
        ],
        out_specs=pl.BlockSpec((L, BW), lambda b, c: (b * nc + c, 0)),
        out_shape=jax.ShapeDtypeStruct((T, BW), BF16),
        scratch_shapes=[
            pltpu.VMEM((A_HEADS, A_HEAD_DIM + SUBLANES, A_HEAD_DIM), F32),
            pltpu.VMEM((A_HEADS, 1, 1), F32),
            pltpu.VMEM((A_HEADS, 1, 1), F32),
            pltpu.VMEM((A_HEADS, SUBLANES, L), F32),
            pltpu.VMEM((A_HEADS, L, L), F32),
            pltpu.VMEM((A_HEADS, A_HEAD_DIM, L), F32),
        ],
        compiler_params=_params(("parallel", "arbitrary")),
        name="mlstm",
    )(proj, proj, vt, proj, gcol, grow, norm_a.reshape(1, BW))


def _pool_kernel(u_ref, wp_ref, ps_ref, out_ref, *, S):
    t = lax.broadcasted_iota(jnp.int32, (S, 1), 0)
    gd = B_GROUP_DIM
    for gi, w in enumerate(B_WINDOWS):
        cs = slice(gi * gd, (gi + 1) * gd)
        u = u_ref[:, cs].astype(F32)
        s = u
        k = 1
        while k < w:
            s = s + jnp.where(t >= k, pltpu.roll(s, k, axis=0), 0.0)
            k *= 2
        cnt = jnp.minimum(t + 1, w).astype(F32)
        pooled = s / cnt - u
        y = _dot(pooled.astype(BF16), wp_ref[0, gi]) * ps_ref[:, cs]
        out_ref[:, cs] = y.astype(out_ref.dtype)


def _pool(proj, w_pool, layer, pool_scale, B, S):
    T = proj.shape[0]
    G = len(B_WINDOWS)
    return pl.pallas_call(
        functools.partial(_pool_kernel, S=S),
        grid=(B,),
        in_specs=[
            pl.BlockSpec((S, BW), lambda b: (b, PB_POOL)),
            pl.BlockSpec((1, G, B_GROUP_DIM, B_GROUP_DIM), lambda b: (layer, 0, 0, 0)),
            pl.BlockSpec((1, BW), lambda b: (0, 0)),
        ],
        out_specs=pl.BlockSpec((S, BW), lambda b: (b, 0)),
        out_shape=jax.ShapeDtypeStruct((T, BW), BF16),
        compiler_params=_params(("parallel",)),
        name="pool",
    )(proj, w_pool, pool_scale.reshape(1, BW))


def _vt_kernel(x_ref, wt_ref, o_ref, *, S):
    rm = min(IN_RM, S)

    def body(mi, carry):
        r0 = pl.multiple_of(mi * rm, rm)
        o_ref[0, :, pl.ds(r0, rm)] = _dot_nt(wt_ref[0], x_ref[pl.ds(r0, rm), :]).astype(o_ref.dtype)
        return carry

    lax.fori_loop(0, S // rm, body, 0, unroll=2)


def _vt(xb, w_pack, layer, B, S):
    K = xb.shape[1]
    nb = w_pack.shape[1] // BW - N_MAIN // BW
    return pl.pallas_call(
        functools.partial(_vt_kernel, S=S),
        grid=(B, nb),
        in_specs=[
            pl.BlockSpec((S, K), lambda b, j: (b, 0)),
            pl.BlockSpec((1, BW, K), lambda b, j: (layer, N_MAIN // BW + j, 0)),
        ],
        out_specs=pl.BlockSpec((1, BW, S), lambda b, j: (b, j, 0)),
        out_shape=jax.ShapeDtypeStruct((B, nb * BW, S), BF16),
        compiler_params=_params(("parallel", "arbitrary")),
        name="vt",
    )(xb, w_pack)


def _fox_kernel(q_ref, k_ref, vt_ref, gcol_ref, grow_ref, out_ref, m_ref, l_ref, a_ref, acc_ref,
                s_ref, p_ref, *, tq):
    qi = pl.program_id(1)
    dh = C_HEAD_DIM
    scale = dh ** -0.5
    n_a = 2 * A_HEADS
    m_ref[...] = jnp.full(m_ref.shape, -jnp.inf, F32)
    l_ref[...] = jnp.zeros(l_ref.shape, F32)
    acc_ref[...] = jnp.zeros(acc_ref.shape, F32)

    def block(kj, masked):
        k0 = pl.multiple_of(kj * tq, tq)
        for h in range(C_HEADS):
            cs = slice(h * dh, (h + 1) * dh)
            s_ref[h] = _dot_nt(k_ref[pl.ds(k0, tq), cs], q_ref[:, cs])
        for h in range(C_HEADS):
            c_key = gcol_ref[pl.ds(k0, tq), n_a + h:n_a + h + 1]
            c_qry = grow_ref[0, n_a + h:n_a + h + 1, :]
            lg = s_ref[h] * (scale * LOG2E) + (c_qry * LOG2E - c_key * LOG2E)
            if masked:
                si = lax.broadcasted_iota(jnp.int32, (tq, tq), 0)
                ti = lax.broadcasted_iota(jnp.int32, (tq, tq), 1)
                lg = jnp.where(si <= ti, lg, -jnp.inf)
            s_ref[h] = lg
            m = m_ref[h]
            m_new = jnp.maximum(m, jnp.max(lg, axis=0, keepdims=True))
            a_ref[h] = jnp.exp2(m - m_new)
            m_ref[h] = m_new
        for h in range(C_HEADS):
            p = jnp.exp2(s_ref[h] - m_ref[h])
            l_ref[h] = a_ref[h] * l_ref[h] + jnp.sum(p, axis=0, keepdims=True)
            p_ref[h] = p.astype(BF16)
        for h in range(C_HEADS):
            cs = slice(h * dh, (h + 1) * dh)
            vt = vt_ref[0, cs, pl.ds(k0, tq)]
            acc_ref[h] = a_ref[h] * acc_ref[h] + _dot(vt, p_ref[h])

    def body(kj, carry):
        block(kj, False)
        return carry

    lax.fori_loop(0, qi, body, 0)
    block(qi, True)
    for h in range(C_HEADS):
        cs = slice(h * dh, (h + 1) * dh)
        out_ref[:, cs] = (acc_ref[h] / l_ref[h]).T.astype(out_ref.dtype)


def _fox(proj, vt, gcol, grow, B, S):
    T = proj.shape[0]
    tq = min(FOX_TQ, S)
    nq = S // tq
    return pl.pallas_call(
        functools.partial(_fox_kernel, tq=tq),
        grid=(B, nq),
        in_specs=[
            pl.BlockSpec((tq, BW), lambda b, i: (b * nq + i, PB_CQ)),
            pl.BlockSpec((S, BW), lambda b, i: (b, PB_CK)),
            pl.BlockSpec((1, BW, S), lambda b, i: (b, VT_C, 0)),
            pl.BlockSpec((S, GATE_LANES), lambda b, i: (b, 0)),
            pl.BlockSpec((1, 2 * SUBLANES, tq), lambda b, i: (b, 0, i)),
        ],
        out_specs=pl.BlockSpec((tq, BW), lambda b, i: (b * nq + i, 0)),
        out_shape=jax.ShapeDtypeStruct((T, BW), BF16),
        scratch_shapes=[
            pltpu.VMEM((C_HEADS, 1, tq), F32),
            pltpu.VMEM((C_HEADS, 1, tq), F32),
            pltpu.VMEM((C_HEADS, 1, tq), F32),
            pltpu.VMEM((C_HEADS, C_HEAD_DIM, tq), F32),
            pltpu.VMEM((C_HEADS, tq, tq), F32),
            pltpu.VMEM((C_HEADS, tq, tq), BF16),
        ],
        compiler_params=_params(("parallel", "arbitrary")),
        name="fox",
    )(proj, proj, vt, gcol, grow)


def _merge_kernel(ha_ref, hb_ref, hc_ref, g0_ref, g1_ref, g2_ref, wb_ref, wo_ref, x_ref,
                  pg_ref, pb_ref, lg_ref, lb_ref, xe_ref, *, pre_norm):
    merged = g0_ref[...].astype(F32) * _dot(ha_ref[...], wb_ref[0, 0])
    merged = merged + g1_ref[...].astype(F32) * _dot(hb_ref[...], wb_ref[0, 1])
    merged = merged + g2_ref[...].astype(F32) * _dot(hc_ref[...], wb_ref[0, 2])
    h = _dot(merged.astype(BF16), wo_ref[0])
    d = x_ref.shape[1]
    x = x_ref[...]
    if pre_norm:
        x = _ln_rows(x, pg_ref[...], pb_ref[...])
    xe_ref[:, 0:d] = _ln_rows(ALPHA * x + h, lg_ref[...], lb_ref[...])
    xe_ref[:, d:] = jnp.zeros((xe_ref.shape[0], xe_ref.shape[1] - d), F32)


def _merge(ha, hb, hc, proj, w_branch, w_out, layer, x, pre_norm, pre_g, pre_b, ln_g, ln_b):
    T, D = x.shape
    tm = MERGE_TM
    hblk = pl.BlockSpec((tm, BW), lambda i: (i, 0))
    gblk = lambda n: pl.BlockSpec((tm, D), lambda i, n=n: (i, PB_GATES * BW // D + n))
    const = lambda shape: pl.BlockSpec(shape, lambda i: (0,) * len(shape), pipeline_mode=pl.Buffered(1))
    of_layer = lambda shape: pl.BlockSpec((1,) + shape, lambda i: (layer,) + (0,) * len(shape),
                                          pipeline_mode=pl.Buffered(1))
    row = pl.BlockSpec((tm, D), lambda i: (i, 0))
    return pl.pallas_call(
        functools.partial(_merge_kernel, pre_norm=pre_norm),
        grid=(T // tm,),
        in_specs=[hblk, hblk, hblk, gblk(0), gblk(1), gblk(2),
                  of_layer((N_BRANCH, BW, D)), of_layer((D, D)), row,
                  const((1, D)), const((1, D)), const((1, D)), const((1, D))],
        out_specs=pl.BlockSpec((tm, D + LANES), lambda i: (i, 0)),
        out_shape=jax.ShapeDtypeStruct((T, D + LANES), F32),
        compiler_params=_params(("parallel",)),
        name="merge",
    )(ha, hb, hc, proj, proj, proj, w_branch, w_out, x, pre_g.reshape(1, D), pre_b.reshape(1, D),
      ln_g.reshape(1, D), ln_b.reshape(1, D))


def _router_kernel(x_ref, wr_ref, br_ref, oi_ref, cnt_ref, we_ref):
    tm = x_ref.shape[0]
    ng, ne = N_GROUPS, EXPERTS_PER_GROUP
    def split(a):
        hi = a.astype(BF16)
        return hi, (a - hi.astype(F32)).astype(BF16)

    x_hi, x_lo = split(x_ref[...])
    w_hi, w_lo = split(wr_ref[...])
    lt = _dot_nt(w_hi, x_hi) + (_dot_nt(w_hi, x_lo) + _dot_nt(w_lo, x_hi))
    logit = [lt[ng * j:ng * (j + 1), :] for j in range(ne)]
    mx = functools.reduce(jnp.maximum, logit)
    mx = jnp.max(mx, axis=0, keepdims=True)
    ex = [jnp.exp(a - mx) for a in logit]
    den = jnp.sum(functools.reduce(jnp.add, ex), axis=0, keepdims=True)
    prob = [e / den for e in ex]
    sel = [prob[j] + br_ref[ng * j:ng * (j + 1), :] for j in range(ne)]

    def first_argmax(vals):
        best = functools.reduce(jnp.maximum, vals)
        idx = jnp.full(best.shape, float(ne - 1), F32)
        for j in range(ne - 2, -1, -1):
            idx = jnp.where(vals[j] == best, float(j), idx)
        return best, idx

    def pick(vals, idx):
        out = vals[ne - 1]
        for j in range(ne - 2, -1, -1):
            out = jnp.where(idx == float(j), vals[j], out)
        return out

    m1, i1 = first_argmax(sel)
    rest = [jnp.where(i1 == float(j), -jnp.inf, sel[j]) for j in range(ne)]
    m2, i2 = first_argmax(rest)
    score = m1 + m2
    gi = lax.broadcasted_iota(jnp.int32, (ng, tm), 0).astype(F32)
    gidx = jnp.min(jnp.where(score == jnp.max(score, axis=0, keepdims=True), gi, float(ng)),
                   axis=0, keepdims=True)
    chosen = gi == gidx
    take = lambda a: jnp.sum(jnp.where(chosen, a, 0.0), axis=0, keepdims=True)
    e1, e2 = take(i1), take(i2)
    p1, p2 = take(pick(prob, i1)), take(pick(prob, i2))
    psum = p1 + p2
    p1, p2 = p1 / psum, p2 / psum
    lo, hi = jnp.minimum(e1, e2), jnp.maximum(e1, e2)
    last = float(N_PAIRS - 1)
    pair = jnp.where(lo == 0.0, hi - 1.0, jnp.where(lo == 1.0, 6.0 - hi, last))
    w_lo = jnp.where(e1 < e2, p1, p2)
    w_hi = jnp.where(e1 < e2, p2, p1)
    w_a = jnp.where(pair == last, w_hi, w_lo)
    w_b = jnp.where(pair == last, w_lo, w_hi)
    cls = gidx * N_PAIRS + pair
    @pl.when(pl.program_id(0) == 0)
    def _():
        cnt_ref[...] = jnp.zeros_like(cnt_ref)

    onehot = lax.broadcasted_iota(jnp.int32, (N_CLASSES, tm), 0).astype(F32) == cls
    jr = lax.broadcasted_iota(jnp.int32, (tm, tm), 0)
    tc = lax.broadcasted_iota(jnp.int32, (tm, tm), 1)
    upper = jnp.where(jr <= tc, 1.0, 0.0).astype(BF16)
    cum = _dot(jnp.where(onehot, 1.0, 0.0).astype(BF16), upper)
    before = cnt_ref[:, 0:1]
    rank = jnp.sum(jnp.where(onehot, cum - 1.0 + before, 0.0), axis=0, keepdims=True)
    cnt_ref[...] = cnt_ref[...] + cum[:, tm - 1:tm]
    zero = jnp.zeros((1, tm), F32)
    rows_i = [cls, gidx * ne + lo, gidx * ne + hi, rank] + [zero] * (SUBLANES - 4)
    oi_ref[...] = jnp.concatenate(rows_i, axis=0).astype(jnp.int32)
    extra = jnp.concatenate([w_a, w_b, jnp.zeros((LANES - 2, tm), F32)], axis=0)
    we_ref[...] = extra.T


def _router(xe, wr_t, br_t):
    T = xe.shape[0]
    D = D_MODEL
    tm = min(ROUTER_TM, T)
    return pl.pallas_call(
        _router_kernel,
        grid=(T // tm,),
        in_specs=[
            pl.BlockSpec((tm, D), lambda i: (i, 0)),
            pl.BlockSpec((N_EXPERTS, D), lambda i: (0, 0)),
            pl.BlockSpec((N_EXPERTS, 1), lambda i: (0, 0)),
        ],
        out_specs=[
            pl.BlockSpec((SUBLANES, tm), lambda i: (0, i)),
            pl.BlockSpec((N_CLASSES, LANES), lambda i: (0, 0)),
            pl.BlockSpec((tm, LANES), lambda i: (i, D // LANES)),
        ],
        out_shape=[
            jax.ShapeDtypeStruct((SUBLANES, T), jnp.int32),
            jax.ShapeDtypeStruct((N_CLASSES, LANES), F32),
            jax.ShapeDtypeStruct((T, D + LANES), F32),
        ],
        input_output_aliases={0: 2},
        compiler_params=_params(("arbitrary",)),
        name="router",
    )(xe, wr_t, br_t)


def _moe_kernel(ea_ref, eb_ref, nrows_ref, start_ref, order_ref, x_hbm, w13a_ref, w13b_ref,
                w2a_ref, w2b_ref, lg_ref, lb_ref, out_hbm, xbuf, obuf, sem_in, sem_out, *, tm, nt):
    i = pl.program_id(0)
    slot = i % 2
    n_cur = nrows_ref[i]
    i_next = jnp.minimum(i + 1, nt - 1)
    n_next = nrows_ref[i_next]
    n_prev = nrows_ref[jnp.maximum(i - 1, 0)]
    has_next = (i + 1 < nt) & (n_next > 0)
    has_prev = (i > 0) & (n_prev > 0)

    def gather(tile, s):
        base = start_ref[tile]

        def start(r, c):
            tok = order_ref[base + r]
            pltpu.make_async_copy(x_hbm.at[pl.ds(tok, 1)], xbuf.at[s, pl.ds(r, 1)],
                                  sem_in.at[s]).start()
            return c

        lax.fori_loop(0, tm, start, 0, unroll=MOE_DMA_UNROLL)

    def wait_gather(s):
        pltpu.make_async_copy(x_hbm.at[pl.ds(0, tm)], xbuf.at[s], sem_in.at[s]).wait()

    def wait_scatter(n):
        p = tm
        while p >= 1:
            @pl.when((n & p) != 0)
            def _(p=p):
                pltpu.make_async_copy(obuf.at[pl.ds(0, p)], out_hbm.at[pl.ds(0, p)], sem_out).wait()

            p //= 2

    @pl.when((i == 0) & (n_cur > 0))
    def _():
        gather(i, 0)

    @pl.when(n_cur > 0)
    def _():
        wait_gather(slot)

    @pl.when(has_next)
    def _():
        gather(i_next, 1 - slot)

    @pl.when(n_cur > 0)
    def _():
        d = out_hbm.shape[1]
        x = xbuf[slot, :, 0:d]
        w_a = xbuf[slot, :, d:d + 1]
        w_b = xbuf[slot, :, d + 1:d + 2]
        xb = x.astype(BF16)

        def expert(w13_ref, w2_ref):
            h = _dot(xb, w13_ref[0, 0])
            a = jax.nn.silu(h[:, :D_FF]) * h[:, D_FF:]
            return _dot(a.astype(BF16), w2_ref[0, 0])

        y = w_a * expert(w13a_ref, w2a_ref)
        y = y + w_b * expert(w13b_ref, w2b_ref)
        z = _ln_rows(ALPHA * x + y, lg_ref[...], lb_ref[...])

        @pl.when(has_prev)
        def _():
            wait_scatter(n_prev)

        obuf[...] = z

        base = start_ref[i]

        def start_out(r, c):
            tok = order_ref[base + r]
            pltpu.make_async_copy(obuf.at[pl.ds(r, 1)], out_hbm.at[pl.ds(tok, 1)], sem_out).start()
            return c

        @pl.when(n_cur == tm)
        def _():
            lax.fori_loop(0, tm, start_out, 0, unroll=MOE_DMA_UNROLL)

        @pl.when(n_cur < tm)
        def _():
            lax.fori_loop(0, n_cur, start_out, 0)

    @pl.when((n_cur == 0) & has_prev)
    def _():
        wait_scatter(n_prev)

    @pl.when((i == nt - 1) & (n_cur > 0))
    def _():
        wait_scatter(n_cur)


def _moe(xe, oi, cnt, w13, w2, layer, ln_g, ln_b):
    T = xe.shape[0]
    D = D_MODEL
    tm = MOE_TM
    npad = T + N_CLASSES * tm
    nt = npad // tm
    cls, rank = oi[0], oi[3]
    counts = cnt[:, 0].astype(jnp.int32)
    ptiles = (counts + tm - 1) // tm
    tile_end = jnp.cumsum(ptiles)
    tile_off = tile_end - ptiles
    row_off = jnp.cumsum(counts) - counts
    order = jnp.argsort(cls * T + rank).astype(jnp.int32)
    order = jnp.concatenate([order, jnp.zeros((tm,), jnp.int32)])
    tile_id = jnp.arange(nt, dtype=jnp.int32)
    n_used = tile_end[-1]
    tile_cls = jnp.searchsorted(tile_end, jnp.minimum(tile_id, n_used - 1), side="right").astype(jnp.int32)
    in_class_tile = (tile_id - tile_off[tile_cls]) * tm
    tile_rows = jnp.clip(counts[tile_cls] - in_class_tile, 0, tm)
    tile_rows = jnp.where(tile_id < n_used, tile_rows, 0).astype(jnp.int32)
    tile_start = jnp.where(tile_id < n_used, row_off[tile_cls] + in_class_tile, 0).astype(jnp.int32)
    pair = tile_cls % N_PAIRS
    tile_e1 = (tile_cls // N_PAIRS) * EXPERTS_PER_GROUP + jnp.array(PAIR_SLOT_A, jnp.int32)[pair]
    tile_e2 = (tile_cls // N_PAIRS) * EXPERTS_PER_GROUP + jnp.array(PAIR_SLOT_B, jnp.int32)[pair]

    F2 = 2 * D_FF
    grid_spec = pltpu.PrefetchScalarGridSpec(
        num_scalar_prefetch=5,
        grid=(nt,),
        in_specs=[
            pl.BlockSpec(memory_space=pl.ANY),
            pl.BlockSpec((1, 1, D, F2), lambda i, e1, e2, *_: (layer, e1[i], 0, 0)),
            pl.BlockSpec((1, 1, D, F2), lambda i, e1, e2, *_: (layer, e2[i], 0, 0)),
            pl.BlockSpec((1, 1, D_FF, D), lambda i, e1, e2, *_: (layer, e1[i], 0, 0)),
            pl.BlockSpec((1, 1, D_FF, D), lambda i, e1, e2, *_: (layer, e2[i], 0, 0)),
            pl.BlockSpec((1, D), lambda i, *_: (0, 0)),
            pl.BlockSpec((1, D), lambda i, *_: (0, 0)),
        ],
        out_specs=pl.BlockSpec(memory_space=pl.ANY),
        scratch_shapes=[
            pltpu.VMEM((2, tm, D + LANES), F32),
            pltpu.VMEM((tm, D), F32),
            pltpu.SemaphoreType.DMA((2,)),
            pltpu.SemaphoreType.DMA(()),
        ],
    )
    return pl.pallas_call(
        functools.partial(_moe_kernel, tm=tm, nt=nt),
        grid_spec=grid_spec,
        out_shape=jax.ShapeDtypeStruct((T, D), F32),
        compiler_params=_params(("arbitrary",)),
        name="moe",
    )(tile_e1, tile_e2, tile_rows, tile_start, order, xe, w13, w13, w2, w2,
      ln_g.reshape(1, D), ln_b.reshape(1, D))


def _pack_kernel(off_ref, w_ref, o_ref):
    o_ref[...] = w_ref[...].astype(o_ref.dtype)


def _pack_weights(w_in):
    L, K, _ = w_in.shape
    w_t = jnp.swapaxes(w_in, 1, 2)
    segs = ((OFF_A, 2 * BW), (OFF_G, N_G), (OFF_A + 3 * BW, BW), (OFF_B, BW), (OFF_C, 2 * BW),
            (OFF_A + 2 * BW, BW), (OFF_C + 2 * BW, BW))
    offs = [off + k * BW for off, width in segs for k in range(width // BW)]
    nblk = len(offs)
    grid_spec = pltpu.PrefetchScalarGridSpec(
        num_scalar_prefetch=1,
        grid=(L, nblk),
        in_specs=[pl.BlockSpec((pl.Element(1), pl.Element(BW), pl.Element(K)),
                               lambda l, j, off: (l, pl.multiple_of(off[j], SUBLANES), 0))],
        out_specs=pl.BlockSpec((1, BW, K), lambda l, j, off: (l, j, 0)),
    )
    return pl.pallas_call(
        _pack_kernel,
        grid_spec=grid_spec,
        out_shape=jax.ShapeDtypeStruct((L, nblk * BW, K), BF16),
        compiler_params=_params(("parallel", "parallel")),
        name="pack",
    )(jnp.asarray(offs, jnp.int32), w_t)


def _gate_bias_and_conv(conv_a, b_if_a, b_f_c):
    n_gate = 2 * A_HEADS + C_HEADS
    bias = jnp.pad(jnp.concatenate([b_if_a, b_f_c]), (0, GATE_LANES - n_gate)).reshape(1, GATE_LANES)
    k_scale = jnp.concatenate([jnp.ones((BW,), F32), jnp.full((BW,), A_HEAD_DIM ** -0.5, F32)])
    cw = jnp.concatenate([conv_a, k_scale[None, :], jnp.zeros((SUBLANES - A_CONV - 1, 2 * BW), F32)], axis=0)
    cw = jnp.pad(cw, ((0, 0), (0, N_MAIN - 2 * BW)))
    return bias, cw


def _forward(x, ln0_g, ln0_b, w_in, conv_a, b_if_a, norm_a, w_pool, pool_scale, b_f_c,
             w_branch, w_out, ln1_g, ln1_b, w_router, b_router, w13, w2, ln2_g, ln2_b):
    B, S, D = x.shape
    T = B * S
    xf = x.reshape(T, D)
    xb = _ln0(xf, ln0_g, ln0_b)
    perm = (jnp.arange(N_EXPERTS) % N_GROUPS) * EXPERTS_PER_GROUP + jnp.arange(N_EXPERTS) // N_GROUPS
    wr_t = w_router.T[perm]
    br_t = b_router[perm].reshape(N_EXPERTS, 1)
    w_pool_b, w_branch_b, w_out_b = w_pool.astype(BF16), w_branch.astype(BF16), w_out.astype(BF16)
    w13_b, w2_b = w13.astype(BF16), w2.astype(BF16)
    w_pack = _pack_weights(w_in)
    a_end = OFF_A + 4 * BW
    n_gate = 2 * A_HEADS + C_HEADS
    wg_t = jnp.concatenate([jnp.swapaxes(w_in[:, :, a_end:a_end + 2 * A_HEADS], 1, 2),
                            jnp.swapaxes(w_in[:, :, OFF_C + 3 * BW:OFF_C + N_C], 1, 2)], axis=1)
    wg_t = jnp.pad(wg_t, ((0, 0), (0, GATE_LANES - n_gate), (0, 0)))
    for l in range(w_in.shape[0]):
        bias, cw = _gate_bias_and_conv(conv_a[l], b_if_a[l], b_f_c[l])
        proj = _inproj(xb, w_pack, l, cw, S)
        vt = _vt(xb, w_pack, l, B, S)
        gcol, grow = _gates(xb, wg_t, l, bias, B, S)
        ha = _mlstm(proj, vt, gcol, grow, norm_a[l], B, S)
        hb = _pool(proj, w_pool_b, l, pool_scale[l], B, S)
        hc = _fox(proj, vt, gcol, grow, B, S)
        xe = _merge(ha, hb, hc, proj, w_branch_b, w_out_b, l, xf, l == 0, ln0_g, ln0_b,
                    ln1_g[l], ln1_b[l])
        oi, cnt, xe = _router(xe, wr_t, br_t)
        xf = _moe(xe, oi, cnt, w13_b, w2_b, l, ln2_g[l], ln2_b[l])
        xb = xf.astype(BF16)
    return xf.reshape(B, S, D)


def kernel(x, ln0_g, ln0_b, w_in, conv_a, b_if_a, norm_a, w_pool, pool_scale, b_f_c, w_branch,
           w_out, ln1_g, ln1_b, w_router, b_router, w13, w2, ln2_g, ln2_b):
    return _forward(x, ln0_g, ln0_b, w_in, conv_a, b_if_a, norm_a, w_pool, pool_scale, b_f_c,
                    w_branch, w_out, ln1_g, ln1_b, w_router, b_router, w13, w2, ln2_g, ln2_b)
```
